```python
import jax, jax.numpy as jnp
from jax import lax
import numpy as np


D_MODEL = 2048
BATCH = 2
SEQ = 8192
DEPTH = 4

A_WIDTH = D_MODEL // 2
A_HEADS = 8
A_HEAD_DIM = A_WIDTH // A_HEADS
CONV_WIDTH = 4
LRU_C = 8.0
B_HEADS = 4
B_KEY_WIDTH = D_MODEL // 4
B_VAL_WIDTH = D_MODEL // 2
B_DK = B_KEY_WIDTH // B_HEADS
B_DV = B_VAL_WIDTH // B_HEADS
B_RANK = 16
GATE_NORMALIZER = 16.0
C_HEADS = 8
C_EXPAND = 128
C_KEY_WIDTH = C_HEADS * C_EXPAND
C_VAL_WIDTH = D_MODEL // 2
C_DV = C_VAL_WIDTH // C_HEADS
N_BRANCH = 3
CHUNK = 64
D_FF = 5632
EPS = 1e-6
EXP_CLIP = 80.0

IN_SPLITS = (A_WIDTH, A_WIDTH,
             B_KEY_WIDTH, B_KEY_WIDTH, B_VAL_WIDTH, B_VAL_WIDTH,
             B_RANK,
             C_KEY_WIDTH, C_KEY_WIDTH, C_VAL_WIDTH, C_VAL_WIDTH,
             N_BRANCH * D_MODEL)
IN_WIDTH = sum(IN_SPLITS)

kernel_name = 'hybrid_rglru_gla_hgrn2_macaron'


def rms_norm(x, g):
    xf = x.astype(jnp.float32)
    y = xf * lax.rsqrt(jnp.mean(xf * xf, axis=-1, keepdims=True) + EPS)
    return (y * g.astype(jnp.float32)).astype(x.dtype)


def head_rms_norm(o, g):
    return o * lax.rsqrt(jnp.mean(o * o, axis=-1, keepdims=True) + EPS) * g.astype(jnp.float32)


def swiglu(h, w_gate, w_up, w_down):
    return (jax.nn.silu(h @ w_gate) * (h @ w_up)) @ w_down


def causal_depthwise_conv(x, w, b):
    s = x.shape[1]
    xp = jnp.pad(x, ((0, 0), (CONV_WIDTH - 1, 0), (0, 0)))
    out = b
    for k in range(CONV_WIDTH):
        out = out + xp[:, k:k + s] * w[k]
    return out


def rg_lru(x, w_a, b_a, w_x, b_x, lam):
    bsz, s, w = x.shape
    f32 = jnp.float32
    xf = x.astype(f32)
    xh = xf.reshape(bsz, s, A_HEADS, A_HEAD_DIM)
    r = jax.nn.sigmoid(jnp.einsum('bshi,hij->bshj', xh, w_a.astype(f32)).reshape(bsz, s, w) + b_a.astype(f32))
    i = jax.nn.sigmoid(jnp.einsum('bshi,hij->bshj', xh, w_x.astype(f32)).reshape(bsz, s, w) + b_x.astype(f32))
    log_a = -LRU_C * r * jax.nn.softplus(-lam.astype(f32))
    a = jnp.exp(log_a)
    u = jnp.sqrt(-jnp.expm1(2.0 * log_a)) * (i * xf)

    def combine(left, right):
        a1, b1 = left
        a2, b2 = right
        return a1 * a2, a2 * b1 + b2

    _, h = lax.associative_scan(combine, (a, u), axis=1)
    return h.astype(x.dtype)


def chunked_gated_linear_attention(q, k, v, log_f):
    bsz, s, h, dk = q.shape
    dv = v.shape[-1]
    n = s // CHUNK

    def to_chunks(t):
        return t.reshape(bsz, n, CHUNK, h, t.shape[-1]).transpose(1, 0, 3, 2, 4)

    causal = jnp.tril(jnp.ones((CHUNK, CHUNK), dtype=bool))[:, :, None]

    def step(state, inp):
        qc, kc, vc, gc = inp
        b = jnp.cumsum(gc, axis=2)
        diff = b[:, :, :, None, :] - b[:, :, None, :, :]
        decay = jnp.where(causal, jnp.exp(jnp.where(causal, diff, 0.0)), 0.0)
        scores = jnp.einsum('bhid,bhjd,bhijd->bhij', qc, kc, decay)
        o = (jnp.einsum('bhij,bhjv->bhiv', scores, vc)
             + jnp.einsum('bhid,bhdv->bhiv', qc * jnp.exp(b), state))
        b_last = b[:, :, -1:, :]
        state = (state * jnp.exp(b_last[:, :, 0, :, None])
                 + jnp.einsum('bhjd,bhjv->bhdv', kc * jnp.exp(b_last - b), vc))
        return state, o

    s0 = jnp.zeros((bsz, h, dk, dv), jnp.float32)
    _, o = lax.scan(step, s0, (to_chunks(q), to_chunks(k), to_chunks(v), to_chunks(log_f)))
    return o.transpose(1, 0, 3, 2, 4).reshape(bsz, s, h, dv)


def token_mixing(h, lb, w_in, conv_w, conv_b, lru_w_a, lru_b_a, lru_w_x, lru_b_x, lru_lambda,
                 gla_w2, gla_b2, gla_norm_g, hgrn_norm_g, w_br_a, w_br_b, w_br_c, w_out):
    bsz, s, _ = h.shape
    f32 = jnp.float32
    offsets = np.cumsum(IN_SPLITS)[:-1].tolist()
    (a_x, a_gate, b_q, b_k, b_v, b_g, b_lr,
     c_q, c_f, c_i, c_g, merge) = jnp.split(h @ w_in, offsets, axis=-1)

    a_conv = causal_depthwise_conv(a_x, conv_w, conv_b)
    y_a = jax.nn.gelu(a_gate) * rg_lru(a_conv, lru_w_a, lru_b_a, lru_w_x, lru_b_x, lru_lambda)

    q_b = b_q.astype(f32).reshape(bsz, s, B_HEADS, B_DK) * (B_DK ** -0.5)
    k_b = b_k.astype(f32).reshape(bsz, s, B_HEADS, B_DK)
    v_b = b_v.astype(f32).reshape(bsz, s, B_HEADS, B_DV)
    log_alpha = jax.nn.log_sigmoid((b_lr @ gla_w2 + gla_b2).astype(f32)) / GATE_NORMALIZER
    o_b = chunked_gated_linear_attention(q_b, k_b, v_b, log_alpha.reshape(bsz, s, B_HEADS, B_DK))
    y_b = (head_rms_norm(o_b, gla_norm_g).reshape(bsz, s, B_VAL_WIDTH)
           * jax.nn.silu(b_g.astype(f32))).astype(h.dtype)

    q_c = jax.nn.silu(c_q.astype(f32)).reshape(bsz, s, C_HEADS, C_EXPAND)
    f_logit = c_f.astype(f32)
    log_f = jax.nn.log_sigmoid(f_logit) + jnp.log1p(lb * jnp.exp(jnp.minimum(-f_logit, EXP_CLIP)))
    k_c = (1.0 - lb) * jax.nn.sigmoid(-f_logit)
    o_c = chunked_gated_linear_attention(
        q_c, k_c.reshape(bsz, s, C_HEADS, C_EXPAND),
        c_i.astype(f32).reshape(bsz, s, C_HEADS, C_DV), log_f.reshape(bsz, s, C_HEADS, C_EXPAND))
    y_c = (head_rms_norm(o_c, hgrn_norm_g).reshape(bsz, s, C_VAL_WIDTH)
           * jax.nn.sigmoid(c_g.astype(f32))).astype(h.dtype)

    g_a, g_b, g_c = jnp.split(jax.nn.sigmoid(merge), N_BRANCH, axis=-1)
    merged = g_a * (y_a @ w_br_a) + g_b * (y_b @ w_br_b) + g_c * (y_c @ w_br_c)
    return merged @ w_out


def setup_inputs(seed: int = 0) -> dict:
    key = jax.random.key(seed)
    ks = jax.random.split(key, 32)
    f32 = jnp.float32

    def w(k, shape, fan_in):
        return jax.random.normal(k, shape, f32) * (fan_in ** -0.5)

    def gain(k, shape):
        return 1.0 + 0.02 * jax.random.normal(k, shape, f32)

    def bias(k, shape):
        return 0.02 * jax.random.normal(k, shape, f32)

    L = DEPTH
    u = jax.random.uniform(ks[14], (L, A_WIDTH), f32, 0.9, 0.999)
    a0 = u ** (1.0 / LRU_C)
    lru_lambda = jnp.log(a0) - jnp.log1p(-a0)
    return {
        'x': jax.random.normal(ks[0], (BATCH, SEQ, D_MODEL), f32),
        'ffn1_pre_g': gain(ks[1], (L, D_MODEL)),
        'ffn1_w_gate': w(ks[2], (L, D_MODEL, D_FF), D_MODEL),
        'ffn1_w_up': w(ks[3], (L, D_MODEL, D_FF), D_MODEL),
        'ffn1_w_down': w(ks[4], (L, D_FF, D_MODEL), D_FF),
        'ffn1_post_g': gain(ks[5], (L, D_MODEL)),
        'mix_pre_g': gain(ks[6], (L, D_MODEL)),
        'w_in': w(ks[7], (L, D_MODEL, IN_WIDTH), D_MODEL),
        'conv_w': w(ks[8], (L, CONV_WIDTH, A_WIDTH), CONV_WIDTH),
        'conv_b': bias(ks[9], (L, A_WIDTH)),
        'lru_w_a': w(ks[10], (L, A_HEADS, A_HEAD_DIM, A_HEAD_DIM), A_HEAD_DIM),
        'lru_b_a': bias(ks[11], (L, A_WIDTH)),
        'lru_w_x': w(ks[12], (L, A_HEADS, A_HEAD_DIM, A_HEAD_DIM), A_HEAD_DIM),
        'lru_b_x': bias(ks[13], (L, A_WIDTH)),
        'lru_lambda': lru_lambda,
        'gla_w2': w(ks[15], (L, B_RANK, B_KEY_WIDTH), B_RANK),
        'gla_b2': bias(ks[16], (L, B_KEY_WIDTH)),
        'gla_norm_g': gain(ks[17], (L, B_DV)),
        'hgrn_lb_logits': jax.random.normal(ks[18], (L, C_KEY_WIDTH), f32),
        'hgrn_norm_g': gain(ks[19], (L, C_DV)),
        'w_br_a': w(ks[20], (L, A_WIDTH, D_MODEL), A_WIDTH),
        'w_br_b': w(ks[21], (L, B_VAL_WIDTH, D_MODEL), B_VAL_WIDTH),
        'w_br_c': w(ks[22], (L, C_VAL_WIDTH, D_MODEL), C_VAL_WIDTH),
        'w_out': w(ks[23], (L, D_MODEL, D_MODEL), D_MODEL),
        'mix_post_g': gain(ks[24], (L, D_MODEL)),
        'ffn2_pre_g': gain(ks[25], (L, D_MODEL)),
        'ffn2_w_gate': w(ks[26], (L, D_MODEL, D_FF), D_MODEL),
        'ffn2_w_up': w(ks[27], (L, D_MODEL, D_FF), D_MODEL),
        'ffn2_w_down': w(ks[28], (L, D_FF, D_MODEL), D_FF),
        'ffn2_post_g': gain(ks[29], (L, D_MODEL)),
    }


def reference(x, ffn1_pre_g, ffn1_w_gate, ffn1_w_up, ffn1_w_down, ffn1_post_g,
              mix_pre_g, w_in, conv_w, conv_b, lru_w_a, lru_b_a, lru_w_x, lru_b_x, lru_lambda,
              gla_w2, gla_b2, gla_norm_g, hgrn_lb_logits, hgrn_norm_g,
              w_br_a, w_br_b, w_br_c, w_out, mix_post_g,
              ffn2_pre_g, ffn2_w_gate, ffn2_w_up, ffn2_w_down, ffn2_post_g):
    lb_p = jax.nn.softmax(hgrn_lb_logits.astype(jnp.float32), axis=0)
    lb_all = jnp.cumsum(lb_p, axis=0) - lb_p[0:1]
    for l in range(DEPTH):
        x = x + 0.5 * rms_norm(swiglu(rms_norm(x, ffn1_pre_g[l]), ffn1_w_gate[l], ffn1_w_up[l],
                                      ffn1_w_down[l]), ffn1_post_g[l])
        mixed = token_mixing(rms_norm(x, mix_pre_g[l]), lb_all[l], w_in[l], conv_w[l], conv_b[l],
                             lru_w_a[l], lru_b_a[l], lru_w_x[l], lru_b_x[l], lru_lambda[l],
                             gla_w2[l], gla_b2[l], gla_norm_g[l], hgrn_norm_g[l],
                             w_br_a[l], w_br_b[l], w_br_c[l], w_out[l])
        x = x + rms_norm(mixed, mix_post_g[l])
        x = x + 0.5 * rms_norm(swiglu(rms_norm(x, ffn2_pre_g[l]), ffn2_w_gate[l], ffn2_w_up[l],
                                      ffn2_w_down[l]), ffn2_post_g[l])
    return x
```

```python
import functools

import numpy as np
import jax
import jax.numpy as jnp
from jax import lax
from jax.experimental import pallas as pl
from jax.experimental.pallas import tpu as pltpu

F32 = jnp.float32
BF16 = jnp.bfloat16

D_MODEL = 2048
D_FF = 5632
A_WIDTH = 1024
A_HEADS = 8
A_HEAD_DIM = 128
CONV_WIDTH = 4
LRU_C = 8.0
B_HEADS = 4
B_DK = 128
B_DV = 256
B_RANK = 16
GATE_NORMALIZER = 16.0
C_HEADS = 8
C_DK = 128
C_DV = 128
EPS = 1e-6
EXP_CLIP = 80.0

LR_PAD = 512
LR_COL = 5120
PROJ_WIDTH = 15376 - B_RANK + LR_PAD
C_COL = LR_COL + LR_PAD
MERGE_COL = C_COL + 4096

CHUNK = 64
SUB = 16
N_SUB = CHUNK // SUB

VMEM_LIMIT = 48 * 1024 * 1024


def _rms(x, g):
    return x * lax.rsqrt(jnp.mean(x * x, axis=-1, keepdims=True) + EPS) * g


def _sigmoid(x):
    return 1.0 / (1.0 + jnp.exp(-x))


def _softplus(x):
    return jnp.maximum(x, 0.0) + jnp.log1p(jnp.exp(-jnp.abs(x)))


def _log_sigmoid(x):
    return -_softplus(-x)


def _gelu_tanh(x):
    return 0.5 * x * (1.0 + jnp.tanh(np.sqrt(2.0 / np.pi) * (x + 0.044715 * (x * x * x))))


def _dot(a, b):
    return jnp.dot(a, b, preferred_element_type=F32)


def _dot_nt(a, b):
    return lax.dot_general(a, b, (((1,), (1,)), ((), ())), preferred_element_type=F32)


def _dot_tn(a, b):
    return lax.dot_general(a, b, (((0,), (0,)), ((), ())), preferred_element_type=F32)


def _ffn_kernel(x_ref, pre_g_ref, wg_ref, wu_ref, wd_ref, post_g_ref, o_ref, h_ref, *, n_f):
    j = pl.program_id(1)

    @pl.when(j == 0)
    def _():
        h_ref[...] = _rms(x_ref[...], pre_g_ref[...]).astype(BF16)
        o_ref[...] = jnp.zeros_like(o_ref)

    h = h_ref[...]
    g = _dot(h, wg_ref[...])
    u = _dot(h, wu_ref[...])
    act = (g * _sigmoid(g) * u).astype(BF16)
    o_ref[...] += _dot(act, wd_ref[...])

    @pl.when(j == n_f - 1)
    def _():
        o_ref[...] = x_ref[...] + 0.5 * _rms(o_ref[...], post_g_ref[...])


def _ffn(x2d, pre_g, wg, wu, wd, post_g, l, tm, tf):
    t, d = x2d.shape
    f = wg.shape[-1]
    return pl.pallas_call(
        functools.partial(_ffn_kernel, n_f=f // tf),
        grid=(t // tm, f // tf),
        in_specs=[
            pl.BlockSpec((tm, d), lambda i, j: (i, 0)),
            pl.BlockSpec((None, 1, d), lambda i, j: (l, 0, 0)),
            pl.BlockSpec((None, d, tf), lambda i, j: (l, 0, j)),
            pl.BlockSpec((None, d, tf), lambda i, j: (l, 0, j)),
            pl.BlockSpec((None, tf, d), lambda i, j: (l, j, 0)),
            pl.BlockSpec((None, 1, d), lambda i, j: (l, 0, 0)),
        ],
        out_specs=pl.BlockSpec((tm, d), lambda i, j: (i, 0)),
        out_shape=jax.ShapeDtypeStruct((t, d), F32),
        scratch_shapes=[pltpu.VMEM((tm, d), BF16)],
        compiler_params=pltpu.CompilerParams(
            dimension_semantics=("parallel", "arbitrary"), vmem_limit_bytes=VMEM_LIMIT),
        name="ffn",
    )(x2d, pre_g, wg, wu, wd, post_g)


def _proj_kernel(x_ref, g_ref, w_ref, o_ref, h_ref):
    @pl.when(pl.program_id(1) == 0)
    def _():
        h_ref[...] = _rms(x_ref[...], g_ref[...]).astype(BF16)

    o_ref[...] = _dot(h_ref[...], w_ref[...])


def _proj(x2d, g, w, l, tm, tn):
    t, d = x2d.shape
    n = w.shape[-1]
    return pl.pallas_call(
        _proj_kernel,
        grid=(t // tm, n // tn),
        in_specs=[
            pl.BlockSpec((tm, d), lambda i, j: (i, 0)),
            pl.BlockSpec((None, 1, d), lambda i, j: (l, 0, 0)),
            pl.BlockSpec((None, d, tn), lambda i, j: (l, 0, j)),
        ],
        out_specs=pl.BlockSpec((tm, tn), lambda i, j: (i, j)),
        out_shape=jax.ShapeDtypeStruct((t, n), F32),
        scratch_shapes=[pltpu.VMEM((tm, d), BF16)],
        compiler_params=pltpu.CompilerParams(
            dimension_semantics=("parallel", "arbitrary"), vmem_limit_bytes=VMEM_LIMIT),
        name="proj",
    )(x2d, g, w)


def _rglru_kernel(ax_ref, ag_ref, cw_ref, cb_ref, wa_ref, ba_ref, wx_ref, bx_ref, lam_ref,
                  o_ref, xs_ref, h_ref, *, tt):
    halo = 8

    @pl.when(pl.program_id(1) == 0)
    def _():
        xs_ref[0:halo, :] = jnp.zeros((halo, A_WIDTH), F32)
        h_ref[...] = jnp.zeros_like(h_ref)

    xa = ax_ref[...]
    xs_ref[halo:halo + tt, :] = xa
    cw = cw_ref[...]
    conv = cb_ref[...] + cw[3:4, :] * xa
    for k in range(CONV_WIDTH - 1):
        back = CONV_WIDTH - 1 - k
        conv = conv + cw[k:k + 1, :] * xs_ref[halo - back:halo - back + tt, :]
    xs_ref[0:halo, :] = xs_ref[tt:tt + halo, :]

    conv_bf = conv.astype(BF16)
    r_parts, i_parts = [], []
    for hd in range(A_HEADS):
        xh = conv_bf[:, hd * A_HEAD_DIM:(hd + 1) * A_HEAD_DIM]
        r_parts.append(_dot(xh, wa_ref[hd]))
        i_parts.append(_dot(xh, wx_ref[hd]))
    r = _sigmoid(jnp.concatenate(r_parts, axis=1) + ba_ref[...])
    i = _sigmoid(jnp.concatenate(i_parts, axis=1) + bx_ref[...])
    log_a = (-LRU_C * _softplus(-lam_ref[...])) * r
    a = jnp.exp(log_a)
    th = jnp.tanh(log_a)
    u = jnp.sqrt(-2.0 * th / (1.0 - th)) * (i * conv)

    row = lax.broadcasted_iota(jnp.int32, (tt, A_WIDTH), 0)
    s = 1
    while s < tt:
        keep = row >= s
        u = jnp.where(keep, a * pltpu.roll(u, s, 0), 0.0) + u
        a = jnp.where(keep, a * pltpu.roll(a, s, 0), a)
        s *= 2
    h = u + a * h_ref[...]
    h_ref[...] = h[tt - 1:tt, :]
    o_ref[...] = (_gelu_tanh(ag_ref[...]) * h).astype(BF16)


def _rglru(proj3, conv_w, conv_b, wa, ba, wx, bx, lam, l, tt):
    b, s, _ = proj3.shape
    vec = pl.BlockSpec((None, 1, A_WIDTH), lambda bi, ti: (l, 0, 0))
    blk = pl.BlockSpec((None, A_HEADS, A_HEAD_DIM, A_HEAD_DIM), lambda bi, ti: (l, 0, 0, 0))
    return pl.pallas_call(
        functools.partial(_rglru_kernel, tt=tt),
        grid=(b, s // tt),
        in_specs=[
            pl.BlockSpec((None, tt, A_WIDTH), lambda bi, ti: (bi, ti, 0)),
            pl.BlockSpec((None, tt, A_WIDTH), lambda bi, ti: (bi, ti, 1)),
            pl.BlockSpec((None, CONV_WIDTH, A_WIDTH), lambda bi, ti: (l, 0, 0)),
            vec, blk, vec, blk, vec, vec,
        ],
        out_specs=pl.BlockSpec((None, tt, A_WIDTH), lambda bi, ti: (bi, ti, 0)),
        out_shape=jax.ShapeDtypeStruct((b, s, A_WIDTH), BF16),
        scratch_shapes=[pltpu.VMEM((tt + 8, A_WIDTH), F32), pltpu.VMEM((1, A_WIDTH), F32)],
        compiler_params=pltpu.CompilerParams(
            dimension_semantics=("parallel", "arbitrary"), vmem_limit_bytes=VMEM_LIMIT),
        name="rglru",
    )(proj3, proj3, conv_w, conv_b, wa, ba, wx, bx, lam)


def _block_row_bcast(a, j):
    d = a.shape[1]
    return jnp.concatenate(
        [jnp.broadcast_to(a[SUB * blk + j:SUB * blk + j + 1, :], (SUB, d)) for blk in range(N_SUB)],
        axis=0)


def _gla_chunk(q, k, v, g, st_ref, sel, tri, diag_mask, lower_mask):
    g_hi = g.astype(BF16)
    g_r = g - g_hi.astype(F32)
    g_mid = g_r.astype(BF16)
    g_lo = (g_r - g_mid.astype(F32)).astype(BF16)
    b = _dot(tri, g_hi) + _dot(tri, g_mid) + _dot(tri, g_lo)
    b_last = b[CHUNK - 1:CHUNK, :]

    st = st_ref[...]
    o = _dot_nt((q * jnp.exp(b)).astype(BF16), st.astype(BF16))

    off_rows = [jnp.zeros((SUB, CHUNK), F32)]
    for blk in range(1, N_SUB):
        r = b[SUB * blk - 1:SUB * blk, :]
        qt = q[SUB * blk:SUB * (blk + 1), :] * jnp.exp(b[SUB * blk:SUB * (blk + 1), :] - r)
        kt = k * jnp.exp(jnp.minimum(r - b, 0.0))
        off_rows.append(_dot_nt(qt.astype(BF16), kt.astype(BF16)))
    s_off = jnp.concatenate(off_rows, axis=0)

    parts = []
    for j in range(SUB):
        dec = jnp.exp(jnp.minimum(b - _block_row_bcast(b, j), 0.0))
        parts.append((q * _block_row_bcast(k, j) * dec).astype(BF16))
    s_diag = _dot(jnp.concatenate(parts, axis=1), sel)

    scores = jnp.where(diag_mask, s_diag, jnp.where(lower_mask, s_off, 0.0))
    o = o + _dot(scores.astype(BF16), v.astype(BF16))

    khat = k * jnp.exp(b_last - b)
    st_ref[...] = st * jnp.exp(b_last) + _dot_tn(v.astype(BF16), khat.astype(BF16))
    return o


def _chunk_constants():
    row = lax.broadcasted_iota(jnp.int32, (CHUNK, CHUNK), 0)
    col = lax.broadcasted_iota(jnp.int32, (CHUNK, CHUNK), 1)
    tri = jnp.where(row >= col, 1.0, 0.0).astype(BF16)
    diag_mask = (row // SUB == col // SUB) & (col <= row)
    lower_mask = col // SUB < row // SUB
    return tri, diag_mask, lower_mask


def _selector(dk):
    piece = np.arange(SUB * dk)[:, None] // dk
    col = np.arange(CHUNK)[None, :] % SUB
    return jnp.asarray(piece == col, dtype=BF16)


def _gla_loop(q_s, k_s, g_s, v_ref, gate_ref, gate_fn, norm_g, sel_ref, st_ref, o_ref, tt):
    tri, diag_mask, lower_mask = _chunk_constants()
    sel = sel_ref[...]

    def body(c, carry):
        rows = pl.ds(pl.multiple_of(c * CHUNK, CHUNK), CHUNK)
        o = _gla_chunk(q_s[rows, :], k_s[rows, :], v_ref[rows, :], g_s[rows, :],
                       st_ref, sel, tri, diag_mask, lower_mask)
        y = o * lax.rsqrt(jnp.mean(o * o, axis=-1, keepdims=True) + EPS) * norm_g
        o_ref[rows, :] = (y * gate_fn(gate_ref[rows, :])).astype(BF16)
        return carry

    lax.fori_loop(0, tt // CHUNK, body, 0)


def _silu(x):
    return x * _sigmoid(x)


def _gla_b_kernel(q_ref, k_ref, v_ref, gate_ref, lr_ref, w2_ref, b2_ref, ng_ref, sel_ref,
                  o_ref, q_s, k_s, g_s, st_ref, *, tt):
    @pl.when(pl.program_id(2) == 0)
    def _():
        st_ref[...] = jnp.zeros_like(st_ref)

    q_s[...] = q_ref[...] * (B_DK ** -0.5)
    k_s[...] = k_ref[...]
    z = _dot(lr_ref[...].astype(BF16), w2_ref[...]) + b2_ref[...]
    g_s[...] = _log_sigmoid(z) / GATE_NORMALIZER
    _gla_loop(q_s, k_s, g_s, v_ref, gate_ref, _silu, ng_ref[...], sel_ref, st_ref, o_ref, tt)


def _gla_b(proj3, w2, b2, norm_g, sel, l, tt):
    b, s, _ = proj3.shape
    tok = lambda width, base: pl.BlockSpec(
        (None, tt, width), lambda bi, h, ti: (bi, ti, base // width + h))
    return pl.pallas_call(
        functools.partial(_gla_b_kernel, tt=tt),
        grid=(b, B_HEADS, s // tt),
        in_specs=[
            tok(B_DK, 2048), tok(B_DK, 2560), tok(B_DV, 3072), tok(B_DV, 4096),
            pl.BlockSpec((None, tt, 128), lambda bi, h, ti: (bi, ti, LR_COL // 128)),
            pl.BlockSpec((None, 128, B_DK), lambda bi, h, ti: (l, 0, h)),
            pl.BlockSpec((None, 1, B_DK), lambda bi, h, ti: (l, 0, h)),
            pl.BlockSpec((None, 1, B_DV), lambda bi, h, ti: (l, 0, 0)),
            pl.BlockSpec((SUB * B_DK, CHUNK), lambda bi, h, ti: (0, 0)),
        ],
        out_specs=pl.BlockSpec((None, tt, B_DV), lambda bi, h, ti: (bi, ti, h)),
        out_shape=jax.ShapeDtypeStruct((b, s, B_HEADS * B_DV), BF16),
        scratch_shapes=[pltpu.VMEM((tt, B_DK), F32)] * 3 + [pltpu.VMEM((B_DV, B_DK), F32)],
        compiler_params=pltpu.CompilerParams(
            dimension_semantics=("parallel", "parallel", "arbitrary"),
            vmem_limit_bytes=VMEM_LIMIT),
        name="gla_b",
    )(proj3, proj3, proj3, proj3, proj3, w2, b2, norm_g, sel)


def _gla_c_kernel(q_ref, f_ref, v_ref, gate_ref, lbl_ref, ng_ref, sel_ref,
                  o_ref, q_s, k_s, g_s, st_ref, *, tt, layer):
    @pl.when(pl.program_id(2) == 0)
    def _():
        st_ref[...] = jnp.zeros_like(st_ref)

    logits = lbl_ref[...]
    p = jnp.exp(logits - jnp.max(logits, axis=0, keepdims=True))
    p = p / jnp.sum(p, axis=0, keepdims=True)
    lb = jnp.zeros((1, C_DK), F32)
    for m in range(1, layer + 1):
        lb = lb + p[m:m + 1, :]

    z = f_ref[...]
    q_s[...] = _silu(q_ref[...])
    g_s[...] = _log_sigmoid(z) + jnp.log1p(lb * jnp.exp(jnp.minimum(-z, EXP_CLIP)))
    k_s[...] = (1.0 - lb) * _sigmoid(-z)
    _gla_loop(q_s, k_s, g_s, v_ref, gate_ref, _sigmoid, ng_ref[...], sel_ref, st_ref, o_ref, tt)


def _gla_c(proj3, lb_logits, norm_g, sel, l, tt):
    b, s, _ = proj3.shape
    depth = lb_logits.shape[0]
    tok = lambda base: pl.BlockSpec(
        (None, tt, C_DK), lambda bi, h, ti: (bi, ti, base // C_DK + h))
    return pl.pallas_call(
        functools.partial(_gla_c_kernel, tt=tt, layer=l),
        grid=(b, C_HEADS, s // tt),
        in_specs=[
            tok(C_COL), tok(C_COL + 1024), tok(C_COL + 2048), tok(C_COL + 3072),
            pl.BlockSpec((depth, C_DK), lambda bi, h, ti: (0, h)),
            pl.BlockSpec((None, 1, C_DV), lambda bi, h, ti: (l, 0, 0)),
            pl.BlockSpec((SUB * C_DK, CHUNK), lambda bi, h, ti: (0, 0)),
        ],
        out_specs=pl.BlockSpec((None, tt, C_DV), lambda bi, h, ti: (bi, ti, h)),
        out_shape=jax.ShapeDtypeStruct((b, s, C_HEADS * C_DV), BF16),
        scratch_shapes=[pltpu.VMEM((tt, C_DK), F32)] * 3 + [pltpu.VMEM((C_DV, C_DK), F32)],
        compiler_params=pltpu.CompilerParams(
            dimension_semantics=("parallel", "parallel", "arbitrary"),
            vmem_limit_bytes=VMEM_LIMIT),
        name="gla_c",
    )(proj3, proj3, proj3, proj3, lb_logits, norm_g, sel)


def _merge_kernel(ya_ref, yb_ref, yc_ref, ma_ref, mb_ref, mc_ref, wa_ref, wb_ref, wc_ref,
                  wo_ref, x_ref, g_ref, o_ref, *, n_n):
    j = pl.program_id(1)

    @pl.when(j == 0)
    def _():
        o_ref[...] = jnp.zeros_like(o_ref)

    merged = (_sigmoid(ma_ref[...]) * _dot(ya_ref[...], wa_ref[...])
              + _sigmoid(mb_ref[...]) * _dot(yb_ref[...], wb_ref[...])
              + _sigmoid(mc_ref[...]) * _dot(yc_ref[...], wc_ref[...]))
    o_ref[...] += _dot(merged.astype(BF16), wo_ref[...])

    @pl.when(j == n_n - 1)
    def _():
        o_ref[...] = x_ref[...] + _rms(o_ref[...], g_ref[...])


def _merge(ya, yb, yc, proj2, wa, wb, wc, wo, x2d, g, l, tm, tn):
    t, d = x2d.shape
    width = ya.shape[-1]
    n_n = d // tn
    y_spec = pl.BlockSpec((tm, width), lambda i, j: (i, 0))
    gate = lambda k: pl.BlockSpec((tm, tn), lambda i, j: (i, (MERGE_COL + k * d) // tn + j))
    w_spec = pl.BlockSpec((None, width, tn), lambda i, j: (l, 0, j))
    return pl.pallas_call(
        functools.partial(_merge_kernel, n_n=n_n),
        grid=(t // tm, n_n),
        in_specs=[
            y_spec, y_spec, y_spec, gate(0), gate(1), gate(2), w_spec, w_spec, w_spec,
            pl.BlockSpec((None, tn, d), lambda i, j: (l, j, 0)),
            pl.BlockSpec((tm, d), lambda i, j: (i, 0)),
            pl.BlockSpec((None, 1, d), lambda i, j: (l, 0, 0)),
        ],
        out_specs=pl.BlockSpec((tm, d), lambda i, j: (i, 0)),
        out_shape=jax.ShapeDtypeStruct((t, d), F32),
        compiler_params=pltpu.CompilerParams(
            dimension_semantics=("parallel", "arbitrary"), vmem_limit_bytes=VMEM_LIMIT),
        name="merge",
    )(ya, yb, yc, proj2, proj2, proj2, wa, wb, wc, wo, x2d, g)


def _tile(n, want):
    t = min(n, want)
    assert n % t == 0, (n, t)
    return t


def kernel(x, ffn1_pre_g, ffn1_w_gate, ffn1_w_up, ffn1_w_down, ffn1_post_g, mix_pre_g, w_in, conv_w, conv_b, lru_w_a, lru_b_a, lru_w_x, lru_b_x, lru_lambda, gla_w2, gla_b2, gla_norm_g, hgrn_lb_logits, hgrn_norm_g, w_br_a, w_br_b, w_br_c, w_out, mix_post_g, ffn2_pre_g, ffn2_w_gate, ffn2_w_up, ffn2_w_down, ffn2_post_g):
    bsz, seq, d = x.shape
    depth = w_in.shape[0]
    t = bsz * seq
    assert d == D_MODEL and seq % CHUNK == 0

    bf = lambda w: w.astype(BF16)
    row = lambda v: v.reshape(depth, 1, v.shape[-1])
    w_proj = jnp.concatenate(
        [bf(w_in[:, :, :LR_COL + B_RANK]),
         jnp.zeros((depth, d, LR_PAD - B_RANK), BF16),
         bf(w_in[:, :, LR_COL + B_RANK:])], axis=-1)
    w2 = jnp.concatenate(
        [bf(gla_w2), jnp.zeros((depth, 128 - B_RANK, gla_w2.shape[-1]), BF16)], axis=1)
    ffn1 = (row(ffn1_pre_g), bf(ffn1_w_gate), bf(ffn1_w_up), bf(ffn1_w_down), row(ffn1_post_g))
    ffn2 = (row(ffn2_pre_g), bf(ffn2_w_gate), bf(ffn2_w_up), bf(ffn2_w_down), row(ffn2_post_g))
    lru = (conv_w, row(conv_b), bf(lru_w_a), row(lru_b_a), bf(lru_w_x), row(lru_b_x),
           row(lru_lambda))
    br = (bf(w_br_a), bf(w_br_b), bf(w_br_c), bf(w_out))
    sel_b, sel_c = _selector(B_DK), _selector(C_DK)

    tm_ffn, tf = _tile(t, 512), _tile(D_FF, 512)
    tm_proj, tn_proj = _tile(t, 1024), 512
    tm_merge, tn_merge = _tile(t, 512), 512
    tt_lru, tt_gla = _tile(seq, 256), _tile(seq, 512)

    x2 = x.reshape(t, d)
    for l in range(depth):
        x2 = _ffn(x2, *ffn1, l, tm_ffn, tf)
        proj2 = _proj(x2, row(mix_pre_g), w_proj, l, tm_proj, tn_proj)
        proj3 = proj2.reshape(bsz, seq, PROJ_WIDTH)
        y_a = _rglru(proj3, *lru, l, tt_lru)
        y_b = _gla_b(proj3, w2, row(gla_b2), row(gla_norm_g), sel_b, l, tt_gla)
        y_c = _gla_c(proj3, hgrn_lb_logits, row(hgrn_norm_g), sel_c, l, tt_gla)
        x2 = _merge(y_a.reshape(t, -1), y_b.reshape(t, -1), y_c.reshape(t, -1), proj2,
                    *br, x2, row(mix_post_g), l, tm_merge, tn_merge)
        x2 = _ffn(x2, *ffn2, l, tm_ffn, tf)
    return x2.reshape(bsz, seq, d)
```

```python
import functools

import numpy as np
import jax
import jax.numpy as jnp
from jax import lax
from jax.experimental import pallas as pl
from jax.experimental.pallas import tpu as pltpu

F32 = jnp.float32
BF16 = jnp.bfloat16

D_MODEL = 2048
D_FF = 5632
A_WIDTH = 1024
A_HEADS = 8
A_HEAD_DIM = 128
CONV_WIDTH = 4
LRU_C = 8.0
B_HEADS = 4
B_DK = 128
B_DV = 256
B_RANK = 16
GATE_NORMALIZER = 16.0
C_HEADS = 8
C_DK = 128
C_DV = 128
EPS = 1e-6
EXP_CLIP = 80.0
LOG2E = 1.4426950408889634

LR_PAD = 512
LR_COL = 5120
PROJ_WIDTH = 15376 - B_RANK + LR_PAD
C_COL = LR_COL + LR_PAD
MERGE_COL = C_COL + 4096

CHUNK = 64
SUB = 8
N_SUB = CHUNK // SUB
HEADS_PER_STEP = 2

VMEM_LIMIT = 48 * 1024 * 1024


def _rms(x, g):
    return x * lax.rsqrt(jnp.mean(x * x, axis=-1, keepdims=True) + EPS) * g


def _sigmoid(x):
    return 0.5 * jnp.tanh(0.5 * x) + 0.5


def _softplus(x):
    return jnp.maximum(x, 0.0) + jnp.log1p(jnp.exp(-jnp.abs(x)))


def _log_sigmoid(x):
    return -_softplus(-x)


def _gelu_tanh(x):
    return 0.5 * x * (1.0 + jnp.tanh(np.sqrt(2.0 / np.pi) * (x + 0.044715 * (x * x * x))))


def _dot(a, b):
    return jnp.dot(a, b, preferred_element_type=F32)


def _dot_nt(a, b):
    return lax.dot_general(a, b, (((1,), (1,)), ((), ())), preferred_element_type=F32)


def _dot_tn(a, b):
    return lax.dot_general(a, b, (((0,), (0,)), ((), ())), preferred_element_type=F32)


def _ffn_kernel(x_ref, pre_g_ref, wg_ref, wu_ref, wd_ref, post_g_ref, o_ref, h_ref, *, n_f):
    j = pl.program_id(1)

    @pl.when(j == 0)
    def _():
        h_ref[...] = _rms(x_ref[...], pre_g_ref[...]).astype(BF16)
        o_ref[...] = jnp.zeros_like(o_ref)

    h = h_ref[...]
    g = _dot(h, wg_ref[...])
    u = _dot(h, wu_ref[...])
    act = (g * _sigmoid(g) * u).astype(BF16)
    o_ref[...] += _dot(act, wd_ref[...])

    @pl.when(j == n_f - 1)
    def _():
        o_ref[...] = x_ref[...] + 0.5 * _rms(o_ref[...], post_g_ref[...])


def _ffn(x2d, pre_g, wg, wu, wd, post_g, l, tm, tf):
    t, d = x2d.shape
    f = wg.shape[-1]
    return pl.pallas_call(
        functools.partial(_ffn_kernel, n_f=f // tf),
        grid=(t // tm, f // tf),
        in_specs=[
            pl.BlockSpec((tm, d), lambda i, j: (i, 0)),
            pl.BlockSpec((None, 1, d), lambda i, j: (l, 0, 0)),
            pl.BlockSpec((None, d, tf), lambda i, j: (l, 0, j)),
            pl.BlockSpec((None, d, tf), lambda i, j: (l, 0, j)),
            pl.BlockSpec((None, tf, d), lambda i, j: (l, j, 0)),
            pl.BlockSpec((None, 1, d), lambda i, j: (l, 0, 0)),
        ],
        out_specs=pl.BlockSpec((tm, d), lambda i, j: (i, 0)),
        out_shape=jax.ShapeDtypeStruct((t, d), F32),
        scratch_shapes=[pltpu.VMEM((tm, d), BF16)],
        compiler_params=pltpu.CompilerParams(
            dimension_semantics=("parallel", "arbitrary"), vmem_limit_bytes=VMEM_LIMIT),
        name="ffn",
    )(x2d, pre_g, wg, wu, wd, post_g)


def _proj_kernel(x_ref, g_ref, w_ref, o_ref, h_ref):
    @pl.when(pl.program_id(1) == 0)
    def _():
        h_ref[...] = _rms(x_ref[...], g_ref[...]).astype(BF16)

    o_ref[...] = _dot(h_ref[...], w_ref[...])


def _proj(x2d, g, w, l, tm, tn):
    t, d = x2d.shape
    n = w.shape[-1]
    return pl.pallas_call(
        _proj_kernel,
        grid=(t // tm, n // tn),
        in_specs=[
            pl.BlockSpec((tm, d), lambda i, j: (i, 0)),
            pl.BlockSpec((None, 1, d), lambda i, j: (l, 0, 0)),
            pl.BlockSpec((None, d, tn), lambda i, j: (l, 0, j)),
        ],
        out_specs=pl.BlockSpec((tm, tn), lambda i, j: (i, j)),
        out_shape=jax.ShapeDtypeStruct((t, n), F32),
        scratch_shapes=[pltpu.VMEM((tm, d), BF16)],
        compiler_params=pltpu.CompilerParams(
            dimension_semantics=("parallel", "arbitrary"), vmem_limit_bytes=VMEM_LIMIT),
        name="proj",
    )(x2d, g, w)


def _rglru_kernel(ax_ref, ag_ref, cw_ref, cb_ref, wa_ref, ba_ref, wx_ref, bx_ref, lam_ref,
                  o_ref, xs_ref, h_ref, *, tt):
    halo = 8

    @pl.when(pl.program_id(1) == 0)
    def _():
        xs_ref[0:halo, :] = jnp.zeros((halo, A_WIDTH), F32)
        h_ref[...] = jnp.zeros_like(h_ref)

    xa = ax_ref[...]
    xs_ref[halo:halo + tt, :] = xa
    cw = cw_ref[...]
    conv = cb_ref[...] + cw[3:4, :] * xa
    for k in range(CONV_WIDTH - 1):
        back = CONV_WIDTH - 1 - k
        conv = conv + cw[k:k + 1, :] * xs_ref[halo - back:halo - back + tt, :]
    xs_ref[0:halo, :] = xs_ref[tt:tt + halo, :]

    conv_bf = conv.astype(BF16)
    r_parts, i_parts = [], []
    for hd in range(A_HEADS):
        xh = conv_bf[:, hd * A_HEAD_DIM:(hd + 1) * A_HEAD_DIM]
        r_parts.append(_dot(xh, wa_ref[hd]))
        i_parts.append(_dot(xh, wx_ref[hd]))
    r = _sigmoid(jnp.concatenate(r_parts, axis=1) + ba_ref[...])
    i = _sigmoid(jnp.concatenate(i_parts, axis=1) + bx_ref[...])
    log_a = (-LRU_C * _softplus(-lam_ref[...])) * r
    a = jnp.exp(log_a)
    th = jnp.tanh(log_a)
    u = jnp.sqrt(-2.0 * th / (1.0 - th)) * (i * conv)

    row = lax.broadcasted_iota(jnp.int32, (tt, A_WIDTH), 0)
    s = 1
    while s < tt:
        keep = row >= s
        u = jnp.where(keep, a * pltpu.roll(u, s, 0), 0.0) + u
        a = jnp.where(keep, a * pltpu.roll(a, s, 0), a)
        s *= 2
    h = u + a * h_ref[...]
    h_ref[...] = h[tt - 1:tt, :]
    o_ref[...] = (_gelu_tanh(ag_ref[...]) * h).astype(BF16)


def _rglru(proj3, conv_w, conv_b, wa, ba, wx, bx, lam, l, tt):
    b, s, _ = proj3.shape
    vec = pl.BlockSpec((None, 1, A_WIDTH), lambda bi, ti: (l, 0, 0))
    blk = pl.BlockSpec((None, A_HEADS, A_HEAD_DIM, A_HEAD_DIM), lambda bi, ti: (l, 0, 0, 0))
    return pl.pallas_call(
        functools.partial(_rglru_kernel, tt=tt),
        grid=(b, s // tt),
        in_specs=[
            pl.BlockSpec((None, tt, A_WIDTH), lambda bi, ti: (bi, ti, 0)),
            pl.BlockSpec((None, tt, A_WIDTH), lambda bi, ti: (bi, ti, 1)),
            pl.BlockSpec((None, CONV_WIDTH, A_WIDTH), lambda bi, ti: (l, 0, 0)),
            vec, blk, vec, blk, vec, vec,
        ],
        out_specs=pl.BlockSpec((None, tt, A_WIDTH), lambda bi, ti: (bi, ti, 0)),
        out_shape=jax.ShapeDtypeStruct((b, s, A_WIDTH), BF16),
        scratch_shapes=[pltpu.VMEM((tt + 8, A_WIDTH), F32), pltpu.VMEM((1, A_WIDTH), F32)],
        compiler_params=pltpu.CompilerParams(
            dimension_semantics=("parallel", "arbitrary"), vmem_limit_bytes=VMEM_LIMIT),
        name="rglru",
    )(proj3, proj3, conv_w, conv_b, wa, ba, wx, bx, lam)


def _block_row_bcast(ref, r0, j, lanes):
    d = lanes.stop - lanes.start
    return jnp.concatenate(
        [jnp.broadcast_to(ref[r0 + SUB * blk + j:r0 + SUB * blk + j + 1, lanes], (SUB, d))
         for blk in range(N_SUB)], axis=0)


def _cumulative_decay(g_s, b_s, tt, tri):
    for r0 in range(0, tt, CHUNK):
        rows = slice(r0, r0 + CHUNK)
        g2 = g_s[rows, :]
        g_hi = g2.astype(BF16)
        g_r = g2 - g_hi.astype(F32)
        g_mid = g_r.astype(BF16)
        g_lo = (g_r - g_mid.astype(F32)).astype(BF16)
        b_s[rows, :] = _dot(tri, g_hi) + _dot(tri, g_mid) + _dot(tri, g_lo)


def _gla_chunk(q_s, k_s, b_s, r0, lanes, v, st, sel, diag_mask, level_masks):
    rows = slice(r0, r0 + CHUNK)
    q, k, b = q_s[rows, lanes], k_s[rows, lanes], b_s[rows, lanes]
    dk = q.shape[1]
    b_last = b[CHUNK - 1:CHUNK, :]

    o = _dot_nt((q * jnp.exp2(b)).astype(BF16), st.astype(BF16))

    parts = []
    for j in range(SUB):
        dec = jnp.exp2(jnp.minimum(b - _block_row_bcast(b_s, r0, j, lanes), 0.0))
        parts.append((q * _block_row_bcast(k_s, r0, j, lanes) * dec).astype(BF16))
    scores = jnp.where(diag_mask, _dot(jnp.concatenate(parts, axis=1), sel), 0.0)

    size = SUB
    for mask in level_masks:
        q_rows, k_rows = [], []
        zeros = jnp.zeros((size, dk), F32)
        for lo in range(0, CHUNK, 2 * size):
            mid, hi = lo + size, lo + 2 * size
            r = b[mid - 1:mid, :]
            k_rows += [k[lo:mid, :] * jnp.exp2(r - b[lo:mid, :]), zeros]
            q_rows += [zeros, q[mid:hi, :] * jnp.exp2(b[mid:hi, :] - r)]
        s_lvl = _dot_nt(jnp.concatenate(q_rows, axis=0).astype(BF16),
                        jnp.concatenate(k_rows, axis=0).astype(BF16))
        scores = scores + (s_lvl if mask is None else jnp.where(mask, s_lvl, 0.0))
        size *= 2
    o = o + _dot(scores.astype(BF16), v.astype(BF16))

    khat = k * jnp.exp2(b_last - b)
    return o, st * jnp.exp2(b_last) + _dot_tn(v.astype(BF16), khat.astype(BF16))


def _chunk_constants():
    row = lax.broadcasted_iota(jnp.int32, (CHUNK, CHUNK), 0)
    col = lax.broadcasted_iota(jnp.int32, (CHUNK, CHUNK), 1)
    tri = jnp.where(row >= col, 1.0, 0.0).astype(BF16)
    diag_mask = (row // SUB == col // SUB) & (col <= row)
    level_masks = []
    size = SUB
    while 2 * size < CHUNK:
        level_masks.append(row // (2 * size) == col // (2 * size))
        size *= 2
    level_masks.append(None)
    return tri, diag_mask, level_masks


def _selector(dk):
    piece = np.arange(SUB * dk)[:, None] // dk
    col = np.arange(CHUNK)[None, :] % SUB
    return jnp.asarray(piece == col, dtype=BF16)


def _gla_loop(q_s, k_s, g_s, b_s, v_ref, gate_ref, gate_fn, norm_g, sel_ref, st_ref, o_ref,
              tt, dk, dv):
    tri, diag_mask, level_masks = _chunk_constants()
    sel = sel_ref[...]

    _cumulative_decay(g_s, b_s, tt, tri)
    states = [st_ref[h] for h in range(HEADS_PER_STEP)]
    for r0 in range(0, tt, CHUNK):
        rows = slice(r0, r0 + CHUNK)
        for h in range(HEADS_PER_STEP):
            vl = slice(h * dv, (h + 1) * dv)
            o, states[h] = _gla_chunk(q_s, k_s, b_s, r0, slice(h * dk, (h + 1) * dk),
                                      v_ref[rows, vl], states[h], sel, diag_mask, level_masks)
            y = o * lax.rsqrt(jnp.mean(o * o, axis=-1, keepdims=True) + EPS) * norm_g
            o_ref[rows, vl] = (y * gate_fn(gate_ref[rows, vl])).astype(BF16)
    for h in range(HEADS_PER_STEP):
        st_ref[h] = states[h]


def _silu(x):
    return x * _sigmoid(x)


def _gla_b_kernel(q_ref, k_ref, v_ref, gate_ref, lr_ref, w2_ref, b2_ref, ng_ref, sel_ref,
                  o_ref, q_s, k_s, g_s, b_s, st_ref, *, tt):
    @pl.when(pl.program_id(2) == 0)
    def _():
        st_ref[...] = jnp.zeros_like(st_ref)

    q_s[...] = q_ref[...] * (B_DK ** -0.5)
    k_s[...] = k_ref[...]
    z = _dot(lr_ref[...].astype(BF16), w2_ref[...]) + b2_ref[...]
    g_s[...] = _log_sigmoid(z) * (LOG2E / GATE_NORMALIZER)
    _gla_loop(q_s, k_s, g_s, b_s, v_ref, gate_ref, _silu, ng_ref[...], sel_ref, st_ref, o_ref,
              tt, B_DK, B_DV)


def _gla_b(proj3, w2, b2, norm_g, sel, l, tt):
    b, s, _ = proj3.shape
    hp = HEADS_PER_STEP
    tok = lambda width, base: pl.BlockSpec(
        (None, tt, hp * width), lambda bi, h, ti: (bi, ti, base // (hp * width) + h))
    return pl.pallas_call(
        functools.partial(_gla_b_kernel, tt=tt),
        grid=(b, B_HEADS // hp, s // tt),
        in_specs=[
            tok(B_DK, 2048), tok(B_DK, 2560), tok(B_DV, 3072), tok(B_DV, 4096),
            pl.BlockSpec((None, tt, 128), lambda bi, h, ti: (bi, ti, LR_COL // 128)),
            pl.BlockSpec((None, 128, hp * B_DK), lambda bi, h, ti: (l, 0, h)),
            pl.BlockSpec((None, 1, hp * B_DK), lambda bi, h, ti: (l, 0, h)),
            pl.BlockSpec((None, 1, B_DV), lambda bi, h, ti: (l, 0, 0)),
            pl.BlockSpec((SUB * B_DK, CHUNK), lambda bi, h, ti: (0, 0)),
        ],
        out_specs=pl.BlockSpec((None, tt, hp * B_DV), lambda bi, h, ti: (bi, ti, h)),
        out_shape=jax.ShapeDtypeStruct((b, s, B_HEADS * B_DV), BF16),
        scratch_shapes=[pltpu.VMEM((tt, hp * B_DK), F32)] * 4 + [pltpu.VMEM((hp, B_DV, B_DK), F32)],
        compiler_params=pltpu.CompilerParams(
            dimension_semantics=("parallel", "parallel", "arbitrary"),
            vmem_limit_bytes=VMEM_LIMIT),
        name="gla_b",
    )(proj3, proj3, proj3, proj3, proj3, w2, b2, norm_g, sel)


def _gla_c_kernel(q_ref, f_ref, v_ref, gate_ref, lbl_ref, ng_ref, sel_ref,
                  o_ref, q_s, k_s, g_s, b_s, st_ref, *, tt, layer):
    @pl.when(pl.program_id(2) == 0)
    def _():
        st_ref[...] = jnp.zeros_like(st_ref)

    logits = lbl_ref[...]
    p = jnp.exp(logits - jnp.max(logits, axis=0, keepdims=True))
    p = p / jnp.sum(p, axis=0, keepdims=True)
    lb = jnp.zeros((1, logits.shape[1]), F32)
    for m in range(1, layer + 1):
        lb = lb + p[m:m + 1, :]

    z = f_ref[...]
    q_s[...] = _silu(q_ref[...])
    g_s[...] = (_log_sigmoid(z) + jnp.log1p(lb * jnp.exp(jnp.minimum(-z, EXP_CLIP)))) * LOG2E
    k_s[...] = (1.0 - lb) * _sigmoid(-z)
    _gla_loop(q_s, k_s, g_s, b_s, v_ref, gate_ref, _sigmoid, ng_ref[...], sel_ref, st_ref, o_ref,
              tt, C_DK, C_DV)


def _gla_c(proj3, lb_logits, norm_g, sel, l, tt):
    b, s, _ = proj3.shape
    depth = lb_logits.shape[0]
    hp = HEADS_PER_STEP
    tok = lambda base: pl.BlockSpec(
        (None, tt, hp * C_DK), lambda bi, h, ti: (bi, ti, base // (hp * C_DK) + h))
    return pl.pallas_call(
        functools.partial(_gla_c_kernel, tt=tt, layer=l),
        grid=(b, C_HEADS // hp, s // tt),
        in_specs=[
            tok(C_COL), tok(C_COL + 1024), tok(C_COL + 2048), tok(C_COL + 3072),
            pl.BlockSpec((depth, hp * C_DK), lambda bi, h, ti: (0, h)),
            pl.BlockSpec((None, 1, C_DV), lambda bi, h, ti: (l, 0, 0)),
            pl.BlockSpec((SUB * C_DK, CHUNK), lambda bi, h, ti: (0, 0)),
        ],
        out_specs=pl.BlockSpec((None, tt, hp * C_DV), lambda bi, h, ti: (bi, ti, h)),
        out_shape=jax.ShapeDtypeStruct((b, s, C_HEADS * C_DV), BF16),
        scratch_shapes=[pltpu.VMEM((tt, hp * C_DK), F32)] * 4 + [pltpu.VMEM((hp, C_DV, C_DK), F32)],
        compiler_params=pltpu.CompilerParams(
            dimension_semantics=("parallel", "parallel", "arbitrary"),
            vmem_limit_bytes=VMEM_LIMIT),
        name="gla_c",
    )(proj3, proj3, proj3, proj3, lb_logits, norm_g, sel)


def _merge_kernel(ya_ref, yb_ref, yc_ref, ma_ref, mb_ref, mc_ref, wa_ref, wb_ref, wc_ref,
                  wo_ref, x_ref, g_ref, o_ref, *, n_n):
    j = pl.program_id(1)

    @pl.when(j == 0)
    def _():
        o_ref[...] = jnp.zeros_like(o_ref)

    merged = (_sigmoid(ma_ref[...]) * _dot(ya_ref[...], wa_ref[...])
              + _sigmoid(mb_ref[...]) * _dot(yb_ref[...], wb_ref[...])
              + _sigmoid(mc_ref[...]) * _dot(yc_ref[...], wc_ref[...]))
    o_ref[...] += _dot(merged.astype(BF16), wo_ref[...])

    @pl.when(j == n_n - 1)
    def _():
        o_ref[...] = x_ref[...] + _rms(o_ref[...], g_ref[...])


def _merge(ya, yb, yc, proj2, wa, wb, wc, wo, x2d, g, l, tm, tn):
    t, d = x2d.shape
    width = ya.shape[-1]
    n_n = d // tn
    y_spec = pl.BlockSpec((tm, width), lambda i, j: (i, 0))
    gate = lambda k: pl.BlockSpec((tm, tn), lambda i, j: (i, (MERGE_COL + k * d) // tn + j))
    w_spec = pl.BlockSpec((None, width, tn), lambda i, j: (l, 0, j))
    return pl.pallas_call(
        functools.partial(_merge_kernel, n_n=n_n),
        grid=(t // tm, n_n),
        in_specs=[
            y_spec, y_spec, y_spec, gate(0), gate(1), gate(2), w_spec, w_spec, w_spec,
            pl.BlockSpec((None, tn, d), lambda i, j: (l, j, 0)),
            pl.BlockSpec((tm, d), lambda i, j: (i, 0)),
            pl.BlockSpec((None, 1, d), lambda i, j: (l, 0, 0)),
        ],
        out_specs=pl.BlockSpec((tm, d), lambda i, j: (i, 0)),
        out_shape=jax.ShapeDtypeStruct((t, d), F32),
        compiler_params=pltpu.CompilerParams(
            dimension_semantics=("parallel", "arbitrary"), vmem_limit_bytes=VMEM_LIMIT),
        name="merge",
    )(ya, yb, yc, proj2, proj2, proj2, wa, wb, wc, wo, x2d, g)


def _tile(n, want):
    t = min(n, want)
    assert n % t == 0, (n, t)
    return t


def kernel(x, ffn1_pre_g, ffn1_w_gate, ffn1_w_up, ffn1_w_down, ffn1_post_g, mix_pre_g, w_in, conv_w, conv_b, lru_w_a, lru_b_a, lru_w_x, lru_b_x, lru_lambda, gla_w2, gla_b2, gla_norm_g, hgrn_lb_logits, hgrn_norm_g, w_br_a, w_br_b, w_br_c, w_out, mix_post_g, ffn2_pre_g, ffn2_w_gate, ffn2_w_up, ffn2_w_down, ffn2_post_g):
    bsz, seq, d = x.shape
    depth = w_in.shape[0]
    t = bsz * seq
    assert d == D_MODEL and seq % CHUNK == 0

    bf = lambda w: w.astype(BF16)
    row = lambda v: v.reshape(depth, 1, v.shape[-1])
    w_proj = jnp.concatenate(
        [bf(w_in[:, :, :LR_COL + B_RANK]),
         jnp.zeros((depth, d, LR_PAD - B_RANK), BF16),
         bf(w_in[:, :, LR_COL + B_RANK:])], axis=-1)
    w2 = jnp.concatenate(
        [bf(gla_w2), jnp.zeros((depth, 128 - B_RANK, gla_w2.shape[-1]), BF16)], axis=1)
    ffn1 = (row(ffn1_pre_g), bf(ffn1_w_gate), bf(ffn1_w_up), bf(ffn1_w_down), row(ffn1_post_g))
    ffn2 = (row(ffn2_pre_g), bf(ffn2_w_gate), bf(ffn2_w_up), bf(ffn2_w_down), row(ffn2_post_g))
    lru = (conv_w, row(conv_b), bf(lru_w_a), row(lru_b_a), bf(lru_w_x), row(lru_b_x),
           row(lru_lambda))
    br = (bf(w_br_a), bf(w_br_b), bf(w_br_c), bf(w_out))
    sel_b, sel_c = _selector(B_DK), _selector(C_DK)

    tm_ffn, tf = _tile(t, 512), _tile(D_FF, 512)
    tm_proj, tn_proj = _tile(t, 1024), 512
    tm_merge, tn_merge = _tile(t, 512), 512
    tt_lru, tt_gla = _tile(seq, 256), _tile(seq, 512)

    x2 = x.reshape(t, d)
    for l in range(depth):
        x2 = _ffn(x2, *ffn1, l, tm_ffn, tf)
        proj2 = _proj(x2, row(mix_pre_g), w_proj, l, tm_proj, tn_proj)
        proj3 = proj2.reshape(bsz, seq, PROJ_WIDTH)
        y_a = _rglru(proj3, *lru, l, tt_lru)
        y_b = _gla_b(proj3, w2, row(gla_b2), row(gla_norm_g), sel_b, l, tt_gla)
        y_c = _gla_c(proj3, hgrn_lb_logits, row(hgrn_norm_g), sel_c, l, tt_gla)
        x2 = _merge(y_a.reshape(t, -1), y_b.reshape(t, -1), y_c.reshape(t, -1), proj2,
                    *br, x2, row(mix_post_g), l, tm_merge, tn_merge)
        x2 = _ffn(x2, *ffn2, l, tm_ffn, tf)
    return x2.reshape(bsz, seq, d)
```

```python
import functools

import numpy as np
import jax
import jax.numpy as jnp
from jax import lax
from jax.experimental import pallas as pl
from jax.experimental.pallas import tpu as pltpu

F32 = jnp.float32
BF16 = jnp.bfloat16

D_MODEL = 2048
D_FF = 5632
A_WIDTH = 1024
A_HEADS = 8
A_HEAD_DIM = 128
CONV_WIDTH = 4
LRU_C = 8.0
B_HEADS = 4
B_DK = 128
B_DV = 256
B_RANK = 16
GATE_NORMALIZER = 16.0
C_HEADS = 8
C_DK = 128
C_DV = 128
EPS = 1e-6
EXP_CLIP = 80.0
LOG2E = 1.4426950408889634

LR_PAD = 512
LR_COL = 5120
PROJ_WIDTH = 15376 - B_RANK + LR_PAD
C_COL = LR_COL + LR_PAD
MERGE_COL = C_COL + 4096

CHUNK = 64
SUB = 8
N_SUB = CHUNK // SUB
HEADS_PER_STEP = 2
SCAN_GROUP = 8

VMEM_LIMIT = 48 * 1024 * 1024


def _rms(x, g):
    return x * lax.rsqrt(jnp.mean(x * x, axis=-1, keepdims=True) + EPS) * g


def _sigmoid(x):
    return 0.5 * jnp.tanh(0.5 * x) + 0.5


def _softplus(x):
    return jnp.maximum(x, 0.0) + jnp.log1p(jnp.exp(-jnp.abs(x)))


def _log_sigmoid(x):
    return -_softplus(-x)


def _gelu_tanh(x):
    return 0.5 * x * (1.0 + jnp.tanh(np.sqrt(2.0 / np.pi) * (x + 0.044715 * (x * x * x))))


def _dot(a, b):
    return jnp.dot(a, b, preferred_element_type=F32)


def _dot_nt(a, b):
    return lax.dot_general(a, b, (((1,), (1,)), ((), ())), preferred_element_type=F32)


def _dot_tn(a, b):
    return lax.dot_general(a, b, (((0,), (0,)), ((), ())), preferred_element_type=F32)


def _ffn_kernel(x_ref, pre_g_ref, wg_ref, wu_ref, wd_ref, post_g_ref, o_ref, h_ref, *, n_f):
    j = pl.program_id(1)

    @pl.when(j == 0)
    def _():
        h_ref[...] = _rms(x_ref[...], pre_g_ref[...]).astype(BF16)
        o_ref[...] = jnp.zeros_like(o_ref)

    h = h_ref[...]
    g = _dot(h, wg_ref[...])
    u = _dot(h, wu_ref[...])
    act = (g * _sigmoid(g) * u).astype(BF16)
    o_ref[...] += _dot(act, wd_ref[...])

    @pl.when(j == n_f - 1)
    def _():
        o_ref[...] = x_ref[...] + 0.5 * _rms(o_ref[...], post_g_ref[...])


def _ffn(x2d, pre_g, wg, wu, wd, post_g, l, tm, tf):
    t, d = x2d.shape
    f = wg.shape[-1]
    return pl.pallas_call(
        functools.partial(_ffn_kernel, n_f=f // tf),
        grid=(t // tm, f // tf),
        in_specs=[
            pl.BlockSpec((tm, d), lambda i, j: (i, 0)),
            pl.BlockSpec((None, 1, d), lambda i, j: (l, 0, 0)),
            pl.BlockSpec((None, d, tf), lambda i, j: (l, 0, j)),
            pl.BlockSpec((None, d, tf), lambda i, j: (l, 0, j)),
            pl.BlockSpec((None, tf, d), lambda i, j: (l, j, 0)),
            pl.BlockSpec((None, 1, d), lambda i, j: (l, 0, 0)),
        ],
        out_specs=pl.BlockSpec((tm, d), lambda i, j: (i, 0)),
        out_shape=jax.ShapeDtypeStruct((t, d), F32),
        scratch_shapes=[pltpu.VMEM((tm, d), BF16)],
        compiler_params=pltpu.CompilerParams(
            dimension_semantics=("parallel", "arbitrary"), vmem_limit_bytes=VMEM_LIMIT),
        name="ffn",
    )(x2d, pre_g, wg, wu, wd, post_g)


def _proj_kernel(x_ref, g_ref, w_ref, o_ref, h_ref):
    @pl.when(pl.program_id(1) == 0)
    def _():
        h_ref[...] = _rms(x_ref[...], g_ref[...]).astype(BF16)

    o_ref[...] = _dot(h_ref[...], w_ref[...])


def _proj(x2d, g, w, l, tm, tn):
    t, d = x2d.shape
    n = w.shape[-1]
    return pl.pallas_call(
        _proj_kernel,
        grid=(t // tm, n // tn),
        in_specs=[
            pl.BlockSpec((tm, d), lambda i, j: (i, 0)),
            pl.BlockSpec((None, 1, d), lambda i, j: (l, 0, 0)),
            pl.BlockSpec((None, d, tn), lambda i, j: (l, 0, j)),
        ],
        out_specs=pl.BlockSpec((tm, tn), lambda i, j: (i, j)),
        out_shape=jax.ShapeDtypeStruct((t, n), F32),
        scratch_shapes=[pltpu.VMEM((tm, d), BF16)],
        compiler_params=pltpu.CompilerParams(
            dimension_semantics=("parallel", "arbitrary"), vmem_limit_bytes=VMEM_LIMIT),
        name="proj",
    )(x2d, g, w)


def _rglru_kernel(ax_ref, ag_ref, cw_ref, cb_ref, wa_ref, ba_ref, wx_ref, bx_ref, lam_ref,
                  o_ref, xs_ref, h_ref, *, tt):
    halo = 8

    @pl.when(pl.program_id(1) == 0)
    def _():
        xs_ref[0:halo, :] = jnp.zeros((halo, A_WIDTH), F32)
        h_ref[...] = jnp.zeros_like(h_ref)

    xa = ax_ref[...]
    xs_ref[halo:halo + tt, :] = xa
    cw = cw_ref[...]
    conv = cb_ref[...] + cw[3:4, :] * xa
    for k in range(CONV_WIDTH - 1):
        back = CONV_WIDTH - 1 - k
        conv = conv + cw[k:k + 1, :] * xs_ref[halo - back:halo - back + tt, :]
    xs_ref[0:halo, :] = xs_ref[tt:tt + halo, :]

    conv_bf = conv.astype(BF16)
    r_parts, i_parts = [], []
    for hd in range(A_HEADS):
        xh = conv_bf[:, hd * A_HEAD_DIM:(hd + 1) * A_HEAD_DIM]
        r_parts.append(_dot(xh, wa_ref[hd]))
        i_parts.append(_dot(xh, wx_ref[hd]))
    r = _sigmoid(jnp.concatenate(r_parts, axis=1) + ba_ref[...])
    i = _sigmoid(jnp.concatenate(i_parts, axis=1) + bx_ref[...])
    log_a = (-LRU_C * _softplus(-lam_ref[...])) * r
    a = jnp.exp(log_a)
    th = jnp.tanh(log_a)
    u = jnp.sqrt(-2.0 * th / (1.0 - th)) * (i * conv)

    a = a.reshape(tt // SCAN_GROUP, SCAN_GROUP, A_WIDTH)
    u = u.reshape(tt // SCAN_GROUP, SCAN_GROUP, A_WIDTH)
    row_in_group = lax.broadcasted_iota(jnp.int32, a.shape, 1)
    s = 1
    while s < SCAN_GROUP:
        keep = row_in_group >= s
        u = a * jnp.where(keep, pltpu.roll(u, s, 1), 0.0) + u
        a = a * jnp.where(keep, pltpu.roll(a, s, 1), 1.0)
        s *= 2
    carry = h_ref[...]
    groups = []
    for g in range(tt // SCAN_GROUP):
        hg = u[g] + a[g] * carry
        carry = hg[SCAN_GROUP - 1:SCAN_GROUP, :]
        groups.append(hg)
    h_ref[...] = carry
    o_ref[...] = (_gelu_tanh(ag_ref[...]) * jnp.concatenate(groups, axis=0)).astype(BF16)


def _rglru(proj3, conv_w, conv_b, wa, ba, wx, bx, lam, l, tt):
    b, s, _ = proj3.shape
    vec = pl.BlockSpec((None, 1, A_WIDTH), lambda bi, ti: (l, 0, 0))
    blk = pl.BlockSpec((None, A_HEADS, A_HEAD_DIM, A_HEAD_DIM), lambda bi, ti: (l, 0, 0, 0))
    return pl.pallas_call(
        functools.partial(_rglru_kernel, tt=tt),
        grid=(b, s // tt),
        in_specs=[
            pl.BlockSpec((None, tt, A_WIDTH), lambda bi, ti: (bi, ti, 0)),
            pl.BlockSpec((None, tt, A_WIDTH), lambda bi, ti: (bi, ti, 1)),
            pl.BlockSpec((None, CONV_WIDTH, A_WIDTH), lambda bi, ti: (l, 0, 0)),
            vec, blk, vec, blk, vec, vec,
        ],
        out_specs=pl.BlockSpec((None, tt, A_WIDTH), lambda bi, ti: (bi, ti, 0)),
        out_shape=jax.ShapeDtypeStruct((b, s, A_WIDTH), BF16),
        scratch_shapes=[pltpu.VMEM((tt + 8, A_WIDTH), F32), pltpu.VMEM((1, A_WIDTH), F32)],
        compiler_params=pltpu.CompilerParams(
            dimension_semantics=("parallel", "arbitrary"), vmem_limit_bytes=VMEM_LIMIT),
        name="rglru",
    )(proj3, proj3, conv_w, conv_b, wa, ba, wx, bx, lam)


def _block_row_bcast(ref, r0, j, lanes):
    d = lanes.stop - lanes.start
    return jnp.concatenate(
        [jnp.broadcast_to(ref[r0 + SUB * blk + j:r0 + SUB * blk + j + 1, lanes], (SUB, d))
         for blk in range(N_SUB)], axis=0)


def _cumulative_decay(g_s, b_s, tt, tri):
    for r0 in range(0, tt, CHUNK):
        rows = slice(r0, r0 + CHUNK)
        g2 = g_s[rows, :]
        g_hi = g2.astype(BF16)
        g_r = g2 - g_hi.astype(F32)
        g_mid = g_r.astype(BF16)
        g_lo = (g_r - g_mid.astype(F32)).astype(BF16)
        b_s[rows, :] = _dot(tri, g_hi) + _dot(tri, g_mid) + _dot(tri, g_lo)


def _gla_state_update(k_s, b_s, r0, lanes, v):
    rows = slice(r0, r0 + CHUNK)
    k, b = k_s[rows, lanes], b_s[rows, lanes]
    khat = k * jnp.exp2(b[CHUNK - 1:CHUNK, :] - b)
    return _dot_tn(v.astype(BF16), khat.astype(BF16))


def _gla_chunk(q_s, k_s, b_s, r0, lanes, v, st_bf, diag_mask, level_masks):
    rows = slice(r0, r0 + CHUNK)
    q, k, b = q_s[rows, lanes], k_s[rows, lanes], b_s[rows, lanes]
    dk = q.shape[1]

    o = _dot_nt((q * jnp.exp2(b)).astype(BF16), st_bf)

    col_in_blk = lax.broadcasted_iota(jnp.int32, (CHUNK, CHUNK), 1) % SUB
    scores = jnp.zeros((CHUNK, CHUNK), F32)
    for j in range(SUB):
        dec = jnp.exp2(b - _block_row_bcast(b_s, r0, j, lanes))
        s_j = jnp.sum(q * _block_row_bcast(k_s, r0, j, lanes) * dec, axis=1, keepdims=True)
        scores = jnp.where(col_in_blk == j, s_j, scores)
    scores = jnp.where(diag_mask, scores, 0.0)

    size = SUB
    for mask in level_masks:
        q_rows, k_rows = [], []
        zeros = jnp.zeros((size, dk), F32)
        for lo in range(0, CHUNK, 2 * size):
            mid, hi = lo + size, lo + 2 * size
            r = b[mid - 1:mid, :]
            k_rows += [k[lo:mid, :] * jnp.exp2(r - b[lo:mid, :]), zeros]
            q_rows += [zeros, q[mid:hi, :] * jnp.exp2(b[mid:hi, :] - r)]
        s_lvl = _dot_nt(jnp.concatenate(q_rows, axis=0).astype(BF16),
                        jnp.concatenate(k_rows, axis=0).astype(BF16))
        scores = scores + (s_lvl if mask is None else jnp.where(mask, s_lvl, 0.0))
        size *= 2
    return o + _dot(scores.astype(BF16), v.astype(BF16))


def _chunk_constants():
    row = lax.broadcasted_iota(jnp.int32, (CHUNK, CHUNK), 0)
    col = lax.broadcasted_iota(jnp.int32, (CHUNK, CHUNK), 1)
    tri = jnp.where(row >= col, 1.0, 0.0).astype(BF16)
    diag_mask = (row // SUB == col // SUB) & (col <= row)
    level_masks = []
    size = SUB
    while 2 * size < CHUNK:
        level_masks.append(row // (2 * size) == col // (2 * size))
        size *= 2
    level_masks.append(None)
    return tri, diag_mask, level_masks


def _gla_scratch(tt, dk, dv):
    hp, n_chunks = HEADS_PER_STEP, tt // CHUNK
    return ([pltpu.VMEM((tt, hp * dk), F32)] * 4
            + [pltpu.VMEM((hp, n_chunks, dv, dk), F32), pltpu.VMEM((hp, n_chunks, dv, dk), BF16),
               pltpu.VMEM((hp, dv, dk), F32)])


def _gla_loop(q_s, k_s, g_s, b_s, u_s, sb_s, v_ref, gate_ref, gate_fn, norm_g, st_ref, o_ref,
              tt, dk, dv):
    tri, diag_mask, level_masks = _chunk_constants()
    heads = [(h, slice(h * dk, (h + 1) * dk), slice(h * dv, (h + 1) * dv))
             for h in range(HEADS_PER_STEP)]
    chunks = list(enumerate(range(0, tt, CHUNK)))

    _cumulative_decay(g_s, b_s, tt, tri)
    for c, r0 in chunks:
        for h, kl, vl in heads:
            u_s[h, c] = _gla_state_update(k_s, b_s, r0, kl, v_ref[r0:r0 + CHUNK, vl])
    for h, kl, vl in heads:
        st = st_ref[h]
        for c, r0 in chunks:
            sb_s[h, c] = st.astype(BF16)
            st = st * jnp.exp2(b_s[r0 + CHUNK - 1:r0 + CHUNK, kl]) + u_s[h, c]
        st_ref[h] = st
    for c, r0 in chunks:
        rows = slice(r0, r0 + CHUNK)
        for h, kl, vl in heads:
            o = _gla_chunk(q_s, k_s, b_s, r0, kl, v_ref[rows, vl], sb_s[h, c],
                           diag_mask, level_masks)
            y = o * lax.rsqrt(jnp.mean(o * o, axis=-1, keepdims=True) + EPS) * norm_g
            o_ref[rows, vl] = (y * gate_fn(gate_ref[rows, vl])).astype(BF16)


def _silu(x):
    return x * _sigmoid(x)


def _gla_b_kernel(q_ref, k_ref, v_ref, gate_ref, lr_ref, w2_ref, b2_ref, ng_ref,
                  o_ref, q_s, k_s, g_s, b_s, u_s, sb_s, st_ref, *, tt):
    @pl.when(pl.program_id(2) == 0)
    def _():
        st_ref[...] = jnp.zeros_like(st_ref)

    q_s[...] = q_ref[...] * (B_DK ** -0.5)
    k_s[...] = k_ref[...]
    z = _dot(lr_ref[...].astype(BF16), w2_ref[...]) + b2_ref[...]
    g_s[...] = _log_sigmoid(z) * (LOG2E / GATE_NORMALIZER)
    _gla_loop(q_s, k_s, g_s, b_s, u_s, sb_s, v_ref, gate_ref, _silu, ng_ref[...], st_ref, o_ref,
              tt, B_DK, B_DV)


def _gla_b(proj3, w2, b2, norm_g, l, tt):
    b, s, _ = proj3.shape
    hp = HEADS_PER_STEP
    tok = lambda width, base: pl.BlockSpec(
        (None, tt, hp * width), lambda bi, h, ti: (bi, ti, base // (hp * width) + h))
    return pl.pallas_call(
        functools.partial(_gla_b_kernel, tt=tt),
        grid=(b, B_HEADS // hp, s // tt),
        in_specs=[
            tok(B_DK, 2048), tok(B_DK, 2560), tok(B_DV, 3072), tok(B_DV, 4096),
            pl.BlockSpec((None, tt, 128), lambda bi, h, ti: (bi, ti, LR_COL // 128)),
            pl.BlockSpec((None, 128, hp * B_DK), lambda bi, h, ti: (l, 0, h)),
            pl.BlockSpec((None, 1, hp * B_DK), lambda bi, h, ti: (l, 0, h)),
            pl.BlockSpec((None, 1, B_DV), lambda bi, h, ti: (l, 0, 0)),
        ],
        out_specs=pl.BlockSpec((None, tt, hp * B_DV), lambda bi, h, ti: (bi, ti, h)),
        out_shape=jax.ShapeDtypeStruct((b, s, B_HEADS * B_DV), BF16),
        scratch_shapes=_gla_scratch(tt, B_DK, B_DV),
        compiler_params=pltpu.CompilerParams(
            dimension_semantics=("parallel", "parallel", "arbitrary"),
            vmem_limit_bytes=VMEM_LIMIT),
        name="gla_b",
    )(proj3, proj3, proj3, proj3, proj3, w2, b2, norm_g)


def _gla_c_kernel(q_ref, f_ref, v_ref, gate_ref, lbl_ref, ng_ref,
                  o_ref, q_s, k_s, g_s, b_s, u_s, sb_s, st_ref, *, tt, layer):
    @pl.when(pl.program_id(2) == 0)
    def _():
        st_ref[...] = jnp.zeros_like(st_ref)

    logits = lbl_ref[...]
    p = jnp.exp(logits - jnp.max(logits, axis=0, keepdims=True))
    p = p / jnp.sum(p, axis=0, keepdims=True)
    lb = jnp.zeros((1, logits.shape[1]), F32)
    for m in range(1, layer + 1):
        lb = lb + p[m:m + 1, :]

    z = f_ref[...]
    q_s[...] = _silu(q_ref[...])
    t = jnp.exp(jnp.minimum(-z, EXP_CLIP))
    r = 1.0 / (1.0 + t)
    g_s[...] = (jnp.log((1.0 + lb * t) * r) + jnp.minimum(z + EXP_CLIP, 0.0)) * LOG2E
    k_s[...] = (1.0 - lb) * (t * r)
    _gla_loop(q_s, k_s, g_s, b_s, u_s, sb_s, v_ref, gate_ref, _sigmoid, ng_ref[...], st_ref, o_ref,
              tt, C_DK, C_DV)


def _gla_c(proj3, lb_logits, norm_g, l, tt):
    b, s, _ = proj3.shape
    depth = lb_logits.shape[0]
    hp = HEADS_PER_STEP
    tok = lambda base: pl.BlockSpec(
        (None, tt, hp * C_DK), lambda bi, h, ti: (bi, ti, base // (hp * C_DK) + h))
    return pl.pallas_call(
        functools.partial(_gla_c_kernel, tt=tt, layer=l),
        grid=(b, C_HEADS // hp, s // tt),
        in_specs=[
            tok(C_COL), tok(C_COL + 1024), tok(C_COL + 2048), tok(C_COL + 3072),
            pl.BlockSpec((depth, hp * C_DK), lambda bi, h, ti: (0, h)),
            pl.BlockSpec((None, 1, C_DV), lambda bi, h, ti: (l, 0, 0)),
        ],
        out_specs=pl.BlockSpec((None, tt, hp * C_DV), lambda bi, h, ti: (bi, ti, h)),
        out_shape=jax.ShapeDtypeStruct((b, s, C_HEADS * C_DV), BF16),
        scratch_shapes=_gla_scratch(tt, C_DK, C_DV),
        compiler_params=pltpu.CompilerParams(
            dimension_semantics=("parallel", "parallel", "arbitrary"),
            vmem_limit_bytes=VMEM_LIMIT),
        name="gla_c",
    )(proj3, proj3, proj3, proj3, lb_logits, norm_g)


def _merge_kernel(ya_ref, yb_ref, yc_ref, ma_ref, mb_ref, mc_ref, wa_ref, wb_ref, wc_ref,
                  wo_ref, x_ref, g_ref, o_ref, *, n_n):
    j = pl.program_id(1)

    @pl.when(j == 0)
    def _():
        o_ref[...] = jnp.zeros_like(o_ref)

    merged = (_sigmoid(ma_ref[...]) * _dot(ya_ref[...], wa_ref[...])
              + _sigmoid(mb_ref[...]) * _dot(yb_ref[...], wb_ref[...])
              + _sigmoid(mc_ref[...]) * _dot(yc_ref[...], wc_ref[...]))
    o_ref[...] += _dot(merged.astype(BF16), wo_ref[...])

    @pl.when(j == n_n - 1)
    def _():
        o_ref[...] = x_ref[...] + _rms(o_ref[...], g_ref[...])


def _merge(ya, yb, yc, proj2, wa, wb, wc, wo, x2d, g, l, tm, tn):
    t, d = x2d.shape
    width = ya.shape[-1]
    n_n = d // tn
    y_spec = pl.BlockSpec((tm, width), lambda i, j: (i, 0))
    gate = lambda k: pl.BlockSpec((tm, tn), lambda i, j: (i, (MERGE_COL + k * d) // tn + j))
    w_spec = pl.BlockSpec((None, width, tn), lambda i, j: (l, 0, j))
    return pl.pallas_call(
        functools.partial(_merge_kernel, n_n=n_n),
        grid=(t // tm, n_n),
        in_specs=[
            y_spec, y_spec, y_spec, gate(0), gate(1), gate(2), w_spec, w_spec, w_spec,
            pl.BlockSpec((None, tn, d), lambda i, j: (l, j, 0)),
            pl.BlockSpec((tm, d), lambda i, j: (i, 0)),
            pl.BlockSpec((None, 1, d), lambda i, j: (l, 0, 0)),
        ],
        out_specs=pl.BlockSpec((tm, d), lambda i, j: (i, 0)),
        out_shape=jax.ShapeDtypeStruct((t, d), F32),
        compiler_params=pltpu.CompilerParams(
            dimension_semantics=("parallel", "arbitrary"), vmem_limit_bytes=VMEM_LIMIT),
        name="merge",
    )(ya, yb, yc, proj2, proj2, proj2, wa, wb, wc, wo, x2d, g)


def _tile(n, want):
    t = min(n, want)
    assert n % t == 0, (n, t)
    return t


def kernel(x, ffn1_pre_g, ffn1_w_gate, ffn1_w_up, ffn1_w_down, ffn1_post_g, mix_pre_g, w_in, conv_w, conv_b, lru_w_a, lru_b_a, lru_w_x, lru_b_x, lru_lambda, gla_w2, gla_b2, gla_norm_g, hgrn_lb_logits, hgrn_norm_g, w_br_a, w_br_b, w_br_c, w_out, mix_post_g, ffn2_pre_g, ffn2_w_gate, ffn2_w_up, ffn2_w_down, ffn2_post_g):
    bsz, seq, d = x.shape
    depth = w_in.shape[0]
    t = bsz * seq
    assert d == D_MODEL and seq % CHUNK == 0

    bf = lambda w: w.astype(BF16)
    row = lambda v: v.reshape(depth, 1, v.shape[-1])
    w_proj = jnp.concatenate(
        [bf(w_in[:, :, :LR_COL + B_RANK]),
         jnp.zeros((depth, d, LR_PAD - B_RANK), BF16),
         bf(w_in[:, :, LR_COL + B_RANK:])], axis=-1)
    w2 = jnp.concatenate(
        [bf(gla_w2), jnp.zeros((depth, 128 - B_RANK, gla_w2.shape[-1]), BF16)], axis=1)
    ffn1 = (row(ffn1_pre_g), bf(ffn1_w_gate), bf(ffn1_w_up), bf(ffn1_w_down), row(ffn1_post_g))
    ffn2 = (row(ffn2_pre_g), bf(ffn2_w_gate), bf(ffn2_w_up), bf(ffn2_w_down), row(ffn2_post_g))
    lru = (conv_w, row(conv_b), bf(lru_w_a), row(lru_b_a), bf(lru_w_x), row(lru_b_x),
           row(lru_lambda))
    br = (bf(w_br_a), bf(w_br_b), bf(w_br_c), bf(w_out))

    tm_ffn, tf = _tile(t, 512), _tile(D_FF, 512)
    tm_proj, tn_proj = _tile(t, 1024), 512
    tm_merge, tn_merge = _tile(t, 512), 512
    tt_lru, tt_gla = _tile(seq, 256), _tile(seq, 1024)

    x2 = x.reshape(t, d)
    for l in range(depth):
        x2 = _ffn(x2, *ffn1, l, tm_ffn, tf)
        proj2 = _proj(x2, row(mix_pre_g), w_proj, l, tm_proj, tn_proj)
        proj3 = proj2.reshape(bsz, seq, PROJ_WIDTH)
        y_a = _rglru(proj3, *lru, l, tt_lru)
        y_b = _gla_b(proj3, w2, row(gla_b2), row(gla_norm_g), l, tt_gla)
        y_c = _gla_c(proj3, hgrn_lb_logits, row(hgrn_norm_g), l, tt_gla)
        x2 = _merge(y_a.reshape(t, -1), y_b.reshape(t, -1), y_c.reshape(t, -1), proj2,
                    *br, x2, row(mix_post_g), l, tm_merge, tn_merge)
        x2 = _ffn(x2, *ffn2, l, tm_ffn, tf)
    return x2.reshape(bsz, seq, d)
```

```python
import functools

import numpy as np
import jax
import jax.numpy as jnp
from jax import lax
from jax.experimental import pallas as pl
from jax.experimental.pallas import tpu as pltpu

F32 = jnp.float32
BF16 = jnp.bfloat16

D_MODEL = 2048
D_FF = 5632
A_WIDTH = 1024
A_HEADS = 8
A_HEAD_DIM = 128
CONV_WIDTH = 4
LRU_C = 8.0
B_HEADS = 4
B_DK = 128
B_DV = 256
B_RANK = 16
GATE_NORMALIZER = 16.0
C_HEADS = 8
C_DK = 128
C_DV = 128
EPS = 1e-6
EXP_CLIP = 80.0
LOG2E = 1.4426950408889634

LR_PAD = 512
LR_COL = 5120
PROJ_WIDTH = 15376 - B_RANK + LR_PAD
C_COL = LR_COL + LR_PAD
MERGE_COL = C_COL + 4096

CHUNK = 64
SUB = 8
N_SUB = CHUNK // SUB
HEADS_PER_STEP = 2
SCAN_GROUP = 8

VMEM_LIMIT = 48 * 1024 * 1024
VMEM_LIMIT_MERGE = 58 * 1024 * 1024


def _rms(x, g):
    return x * lax.rsqrt(jnp.mean(x * x, axis=-1, keepdims=True) + EPS) * g


def _sigmoid(x):
    return 0.5 * jnp.tanh(0.5 * x) + 0.5


def _softplus(x):
    return jnp.maximum(x, 0.0) + jnp.log1p(jnp.exp(-jnp.abs(x)))


def _log_sigmoid(x):
    return -_softplus(-x)


def _gelu_tanh(x):
    return 0.5 * x * (1.0 + jnp.tanh(np.sqrt(2.0 / np.pi) * (x + 0.044715 * (x * x * x))))


def _dot(a, b):
    return jnp.dot(a, b, preferred_element_type=F32)


def _dot_nt(a, b):
    return lax.dot_general(a, b, (((1,), (1,)), ((), ())), preferred_element_type=F32)


def _dot_tn(a, b):
    return lax.dot_general(a, b, (((0,), (0,)), ((), ())), preferred_element_type=F32)


def _ffn_kernel(x_ref, pre_g_ref, wg_ref, wu_ref, wd_ref, post_g_ref, *rest, n_f, emit_norm):
    if emit_norm:
        next_g_ref, o_ref, hn_ref, h_ref = rest
    else:
        o_ref, h_ref = rest
    j = pl.program_id(1)

    @pl.when(j == 0)
    def _():
        h_ref[...] = _rms(x_ref[...], pre_g_ref[...]).astype(BF16)
        o_ref[...] = jnp.zeros_like(o_ref)

    h = h_ref[...]
    tf = wg_ref.shape[1]
    acc = None
    for c0 in range(0, tf, tf // 2):
        cols = slice(c0, c0 + tf // 2)
        g = _dot(h, wg_ref[:, cols])
        u = _dot(h, wu_ref[:, cols])
        act = (g * _sigmoid(g) * u).astype(BF16)
        part = _dot(act, wd_ref[cols, :])
        acc = part if acc is None else acc + part
    o_ref[...] += acc

    @pl.when(j == n_f - 1)
    def _():
        y = x_ref[...] + 0.5 * _rms(o_ref[...], post_g_ref[...])
        o_ref[...] = y
        if emit_norm:
            hn_ref[...] = _rms(y, next_g_ref[...]).astype(BF16)


def _ffn(x2d, pre_g, wg, wu, wd, post_g, next_g, l, tm, tf):
    t, d = x2d.shape
    f = wg.shape[-1]
    emit_norm = next_g is not None
    gain = pl.BlockSpec((None, 1, d), lambda i, j: (l, 0, 0))
    tile = pl.BlockSpec((tm, d), lambda i, j: (i, 0))
    return pl.pallas_call(
        functools.partial(_ffn_kernel, n_f=f // tf, emit_norm=emit_norm),
        grid=(t // tm, f // tf),
        in_specs=[
            tile, gain,
            pl.BlockSpec((None, d, tf), lambda i, j: (l, 0, j)),
            pl.BlockSpec((None, d, tf), lambda i, j: (l, 0, j)),
            pl.BlockSpec((None, tf, d), lambda i, j: (l, j, 0)),
            gain,
        ] + ([gain] if emit_norm else []),
        out_specs=(tile, tile) if emit_norm else tile,
        out_shape=((jax.ShapeDtypeStruct((t, d), F32), jax.ShapeDtypeStruct((t, d), BF16))
                   if emit_norm else jax.ShapeDtypeStruct((t, d), F32)),
        scratch_shapes=[pltpu.VMEM((tm, d), BF16)],
        compiler_params=pltpu.CompilerParams(
            dimension_semantics=("parallel", "arbitrary"), vmem_limit_bytes=VMEM_LIMIT),
        name="ffn",
    )(x2d, pre_g, wg, wu, wd, post_g, *((next_g,) if emit_norm else ()))


def _proj_kernel(h_ref, w_ref, o_ref):
    o_ref[...] = _dot(h_ref[...], w_ref[...])


def _proj(h2d, w, l, n, tm, tn):
    t, d = h2d.shape
    return pl.pallas_call(
        _proj_kernel,
        grid=(t // tm, n // tn),
        in_specs=[
            pl.BlockSpec((tm, d), lambda i, j: (i, 0)),
            pl.BlockSpec((None, d, tn), lambda i, j: (l, 0, j)),
        ],
        out_specs=pl.BlockSpec((tm, tn), lambda i, j: (i, j)),
        out_shape=jax.ShapeDtypeStruct((t, n), F32),
        compiler_params=pltpu.CompilerParams(
            dimension_semantics=("parallel", "arbitrary"), vmem_limit_bytes=VMEM_LIMIT),
        name="proj",
    )(h2d, w)


def _rglru_kernel(ax_ref, ag_ref, cw_ref, cb_ref, wa_ref, ba_ref, wx_ref, bx_ref, lam_ref,
                  o_ref, xs_ref, h_ref, *, tt):
    halo = 8

    @pl.when(pl.program_id(1) == 0)
    def _():
        xs_ref[0:halo, :] = jnp.zeros((halo, A_WIDTH), F32)
        h_ref[...] = jnp.zeros_like(h_ref)

    xa = ax_ref[...]
    xs_ref[halo:halo + tt, :] = xa
    cw = cw_ref[...]
    conv = cb_ref[...] + cw[3:4, :] * xa
    for k in range(CONV_WIDTH - 1):
        back = CONV_WIDTH - 1 - k
        conv = conv + cw[k:k + 1, :] * xs_ref[halo - back:halo - back + tt, :]
    xs_ref[0:halo, :] = xs_ref[tt:tt + halo, :]

    conv_bf = conv.astype(BF16)
    r_parts, i_parts = [], []
    for hd in range(A_HEADS):
        xh = conv_bf[:, hd * A_HEAD_DIM:(hd + 1) * A_HEAD_DIM]
        r_parts.append(_dot(xh, wa_ref[hd]))
        i_parts.append(_dot(xh, wx_ref[hd]))
    r = _sigmoid(jnp.concatenate(r_parts, axis=1) + ba_ref[...])
    i = _sigmoid(jnp.concatenate(i_parts, axis=1) + bx_ref[...])
    log_a = (-LRU_C * _softplus(-lam_ref[...])) * r
    a = jnp.exp(log_a)
    th = jnp.tanh(log_a)
    u = jnp.sqrt(-2.0 * th / (1.0 - th)) * (i * conv)

    a = a.reshape(tt // SCAN_GROUP, SCAN_GROUP, A_WIDTH)
    u = u.reshape(tt // SCAN_GROUP, SCAN_GROUP, A_WIDTH)
    row_in_group = lax.broadcasted_iota(jnp.int32, a.shape, 1)
    s = 1
    while s < SCAN_GROUP:
        keep = row_in_group >= s
        u = a * jnp.where(keep, pltpu.roll(u, s, 1), 0.0) + u
        a = a * jnp.where(keep, pltpu.roll(a, s, 1), 1.0)
        s *= 2
    carry = h_ref[...]
    groups = []
    for g in range(tt // SCAN_GROUP):
        hg = u[g] + a[g] * carry
        carry = hg[SCAN_GROUP - 1:SCAN_GROUP, :]
        groups.append(hg)
    h_ref[...] = carry
    o_ref[...] = (_gelu_tanh(ag_ref[...]) * jnp.concatenate(groups, axis=0)).astype(BF16)


def _rglru(proj3, conv_w, conv_b, wa, ba, wx, bx, lam, l, tt):
    b, s, _ = proj3.shape
    vec = pl.BlockSpec((None, 1, A_WIDTH), lambda bi, ti: (l, 0, 0))
    blk = pl.BlockSpec((None, A_HEADS, A_HEAD_DIM, A_HEAD_DIM), lambda bi, ti: (l, 0, 0, 0))
    return pl.pallas_call(
        functools.partial(_rglru_kernel, tt=tt),
        grid=(b, s // tt),
        in_specs=[
            pl.BlockSpec((None, tt, A_WIDTH), lambda bi, ti: (bi, ti, 0)),
            pl.BlockSpec((None, tt, A_WIDTH), lambda bi, ti: (bi, ti, 1)),
            pl.BlockSpec((None, CONV_WIDTH, A_WIDTH), lambda bi, ti: (l, 0, 0)),
            vec, blk, vec, blk, vec, vec,
        ],
        out_specs=pl.BlockSpec((None, tt, A_WIDTH), lambda bi, ti: (bi, ti, 0)),
        out_shape=jax.ShapeDtypeStruct((b, s, A_WIDTH), BF16),
        scratch_shapes=[pltpu.VMEM((tt + 8, A_WIDTH), F32), pltpu.VMEM((1, A_WIDTH), F32)],
        compiler_params=pltpu.CompilerParams(
            dimension_semantics=("parallel", "arbitrary"), vmem_limit_bytes=VMEM_LIMIT),
        name="rglru",
    )(proj3, proj3, conv_w, conv_b, wa, ba, wx, bx, lam)


def _block_row_bcast(ref, r0, j, lanes):
    d = lanes.stop - lanes.start
    return jnp.concatenate(
        [jnp.broadcast_to(ref[r0 + SUB * blk + j:r0 + SUB * blk + j + 1, lanes], (SUB, d))
         for blk in range(N_SUB)], axis=0)


def _cumulative_decay(g_s, b_s, tt, tri):
    for r0 in range(0, tt, CHUNK):
        rows = slice(r0, r0 + CHUNK)
        g2 = g_s[rows, :]
        g_hi = g2.astype(BF16)
        g_r = g2 - g_hi.astype(F32)
        g_mid = g_r.astype(BF16)
        g_lo = (g_r - g_mid.astype(F32)).astype(BF16)
        b_s[rows, :] = _dot(tri, g_hi) + _dot(tri, g_mid) + _dot(tri, g_lo)


def _gla_state_update(k_s, b_s, r0, lanes, v):
    rows = slice(r0, r0 + CHUNK)
    k, b = k_s[rows, lanes], b_s[rows, lanes]
    khat = k * jnp.exp2(b[CHUNK - 1:CHUNK, :] - b)
    return _dot_tn(v.astype(BF16), khat.astype(BF16))


def _gla_chunk(q_s, k_s, b_s, r0, lanes, v, st_bf, diag_mask, level_masks):
    rows = slice(r0, r0 + CHUNK)
    q, k, b = q_s[rows, lanes], k_s[rows, lanes], b_s[rows, lanes]
    dk = q.shape[1]

    o = _dot_nt((q * jnp.exp2(b)).astype(BF16), st_bf)

    col_in_blk = lax.broadcasted_iota(jnp.int32, (CHUNK, CHUNK), 1) % SUB
    scores = jnp.zeros((CHUNK, CHUNK), F32)
    for j in range(SUB):
        dec = jnp.exp2(b - _block_row_bcast(b_s, r0, j, lanes))
        s_j = jnp.sum(q * _block_row_bcast(k_s, r0, j, lanes) * dec, axis=1, keepdims=True)
        scores = jnp.where(col_in_blk == j, s_j, scores)
    scores = jnp.where(diag_mask, scores, 0.0)

    size = SUB
    for mask in level_masks:
        q_rows, k_rows = [], []
        zeros = jnp.zeros((size, dk), F32)
        for lo in range(0, CHUNK, 2 * size):
            mid, hi = lo + size, lo + 2 * size
            r = b[mid - 1:mid, :]
            k_rows += [k[lo:mid, :] * jnp.exp2(r - b[lo:mid, :]), zeros]
            q_rows += [zeros, q[mid:hi, :] * jnp.exp2(b[mid:hi, :] - r)]
        s_lvl = _dot_nt(jnp.concatenate(q_rows, axis=0).astype(BF16),
                        jnp.concatenate(k_rows, axis=0).astype(BF16))
        scores = scores + (s_lvl if mask is None else jnp.where(mask, s_lvl, 0.0))
        size *= 2
    return o + _dot(scores.astype(BF16), v.astype(BF16))


def _chunk_constants():
    row = lax.broadcasted_iota(jnp.int32, (CHUNK, CHUNK), 0)
    col = lax.broadcasted_iota(jnp.int32, (CHUNK, CHUNK), 1)
    tri = jnp.where(row >= col, 1.0, 0.0).astype(BF16)
    diag_mask = (row // SUB == col // SUB) & (col <= row)
    level_masks = []
    size = SUB
    while 2 * size < CHUNK:
        level_masks.append(row // (2 * size) == col // (2 * size))
        size *= 2
    level_masks.append(None)
    return tri, diag_mask, level_masks


def _gla_scratch(tt, dk, dv):
    hp, n_chunks = HEADS_PER_STEP, tt // CHUNK
    return ([pltpu.VMEM((tt, hp * dk), F32)] * 4
            + [pltpu.VMEM((hp, n_chunks, dv, dk), F32), pltpu.VMEM((hp, n_chunks, dv, dk), BF16),
               pltpu.VMEM((hp, dv, dk), F32)])


def _gla_loop(q_s, k_s, g_s, b_s, u_s, sb_s, v_ref, gate_ref, gate_fn, norm_g, st_ref, o_ref,
              tt, dk, dv):
    tri, diag_mask, level_masks = _chunk_constants()
    heads = [(h, slice(h * dk, (h + 1) * dk), slice(h * dv, (h + 1) * dv))
             for h in range(HEADS_PER_STEP)]
    chunks = list(enumerate(range(0, tt, CHUNK)))

    _cumulative_decay(g_s, b_s, tt, tri)
    for c, r0 in chunks:
        for h, kl, vl in heads:
            u_s[h, c] = _gla_state_update(k_s, b_s, r0, kl, v_ref[r0:r0 + CHUNK, vl])
    for h, kl, vl in heads:
        st = st_ref[h]
        for c, r0 in chunks:
            sb_s[h, c] = st.astype(BF16)
            st = st * jnp.exp2(b_s[r0 + CHUNK - 1:r0 + CHUNK, kl]) + u_s[h, c]
        st_ref[h] = st
    for c, r0 in chunks:
        rows = slice(r0, r0 + CHUNK)
        for h, kl, vl in heads:
            o = _gla_chunk(q_s, k_s, b_s, r0, kl, v_ref[rows, vl], sb_s[h, c],
                           diag_mask, level_masks)
            y = o * lax.rsqrt(jnp.mean(o * o, axis=-1, keepdims=True) + EPS) * norm_g
            o_ref[rows, vl] = (y * gate_fn(gate_ref[rows, vl])).astype(BF16)


def _silu(x):
    return x * _sigmoid(x)


def _gla_b_kernel(q_ref, k_ref, v_ref, gate_ref, lr_ref, w2_ref, b2_ref, ng_ref,
                  o_ref, q_s, k_s, g_s, b_s, u_s, sb_s, st_ref, *, tt):
    @pl.when(pl.program_id(2) == 0)
    def _():
        st_ref[...] = jnp.zeros_like(st_ref)

    q_s[...] = q_ref[...] * (B_DK ** -0.5)
    k_s[...] = k_ref[...]
    z = _dot(lr_ref[...].astype(BF16), w2_ref[...]) + b2_ref[...]
    g_s[...] = _log_sigmoid(z) * (LOG2E / GATE_NORMALIZER)
    _gla_loop(q_s, k_s, g_s, b_s, u_s, sb_s, v_ref, gate_ref, _silu, ng_ref[...], st_ref, o_ref,
              tt, B_DK, B_DV)


def _gla_b(proj3, w2, b2, norm_g, l, tt):
    b, s, _ = proj3.shape
    hp = HEADS_PER_STEP
    tok = lambda width, base: pl.BlockSpec(
        (None, tt, hp * width), lambda bi, h, ti: (bi, ti, base // (hp * width) + h))
    return pl.pallas_call(
        functools.partial(_gla_b_kernel, tt=tt),
        grid=(b, B_HEADS // hp, s // tt),
        in_specs=[
            tok(B_DK, 2048), tok(B_DK, 2560), tok(B_DV, 3072), tok(B_DV, 4096),
            pl.BlockSpec((None, tt, 128), lambda bi, h, ti: (bi, ti, LR_COL // 128)),
            pl.BlockSpec((None, 128, hp * B_DK), lambda bi, h, ti: (l, 0, h)),
            pl.BlockSpec((None, 1, hp * B_DK), lambda bi, h, ti: (l, 0, h)),
            pl.BlockSpec((None, 1, B_DV), lambda bi, h, ti: (l, 0, 0)),
        ],
        out_specs=pl.BlockSpec((None, tt, hp * B_DV), lambda bi, h, ti: (bi, ti, h)),
        out_shape=jax.ShapeDtypeStruct((b, s, B_HEADS * B_DV), BF16),
        scratch_shapes=_gla_scratch(tt, B_DK, B_DV),
        compiler_params=pltpu.CompilerParams(
            dimension_semantics=("parallel", "parallel", "arbitrary"),
            vmem_limit_bytes=VMEM_LIMIT),
        name="gla_b",
    )(proj3, proj3, proj3, proj3, proj3, w2, b2, norm_g)


def _gla_c_kernel(q_ref, f_ref, v_ref, gate_ref, lbl_ref, ng_ref,
                  o_ref, q_s, k_s, g_s, b_s, u_s, sb_s, st_ref, *, tt, layer):
    @pl.when(pl.program_id(2) == 0)
    def _():
        st_ref[...] = jnp.zeros_like(st_ref)

    logits = lbl_ref[...]
    p = jnp.exp(logits - jnp.max(logits, axis=0, keepdims=True))
    p = p / jnp.sum(p, axis=0, keepdims=True)
    lb = jnp.zeros((1, logits.shape[1]), F32)
    for m in range(1, layer + 1):
        lb = lb + p[m:m + 1, :]

    z = f_ref[...]
    q_s[...] = _silu(q_ref[...])
    t = jnp.exp(jnp.minimum(-z, EXP_CLIP))
    r = 1.0 / (1.0 + t)
    g_s[...] = (jnp.log((1.0 + lb * t) * r) + jnp.minimum(z + EXP_CLIP, 0.0)) * LOG2E
    k_s[...] = (1.0 - lb) * (t * r)
    _gla_loop(q_s, k_s, g_s, b_s, u_s, sb_s, v_ref, gate_ref, _sigmoid, ng_ref[...], st_ref, o_ref,
              tt, C_DK, C_DV)


def _gla_c(proj3, lb_logits, norm_g, l, tt):
    b, s, _ = proj3.shape
    depth = lb_logits.shape[0]
    hp = HEADS_PER_STEP
    tok = lambda base: pl.BlockSpec(
        (None, tt, hp * C_DK), lambda bi, h, ti: (bi, ti, base // (hp * C_DK) + h))
    return pl.pallas_call(
        functools.partial(_gla_c_kernel, tt=tt, layer=l),
        grid=(b, C_HEADS // hp, s // tt),
        in_specs=[
            tok(C_COL), tok(C_COL + 1024), tok(C_COL + 2048), tok(C_COL + 3072),
            pl.BlockSpec((depth, hp * C_DK), lambda bi, h, ti: (0, h)),
            pl.BlockSpec((None, 1, C_DV), lambda bi, h, ti: (l, 0, 0)),
        ],
        out_specs=pl.BlockSpec((None, tt, hp * C_DV), lambda bi, h, ti: (bi, ti, h)),
        out_shape=jax.ShapeDtypeStruct((b, s, C_HEADS * C_DV), BF16),
        scratch_shapes=_gla_scratch(tt, C_DK, C_DV),
        compiler_params=pltpu.CompilerParams(
            dimension_semantics=("parallel", "parallel", "arbitrary"),
            vmem_limit_bytes=VMEM_LIMIT),
        name="gla_c",
    )(proj3, proj3, proj3, proj3, lb_logits, norm_g)


def _merge_kernel(h_ref, ya_ref, yb_ref, yc_ref, ga_ref, gb_ref, gc_ref, wa_ref, wb_ref, wc_ref,
                  wo_ref, x_ref, g_ref, o_ref, *, n_n):
    j = pl.program_id(1)

    @pl.when(j == 0)
    def _():
        o_ref[...] = jnp.zeros_like(o_ref)

    h = h_ref[...]
    merged = (_sigmoid(_dot(h, ga_ref[...])) * _dot(ya_ref[...], wa_ref[...])
              + _sigmoid(_dot(h, gb_ref[...])) * _dot(yb_ref[...], wb_ref[...])
              + _sigmoid(_dot(h, gc_ref[...])) * _dot(yc_ref[...], wc_ref[...]))
    o_ref[...] += _dot(merged.astype(BF16), wo_ref[...])

    @pl.when(j == n_n - 1)
    def _():
        o_ref[...] = x_ref[...] + _rms(o_ref[...], g_ref[...])


def _merge(h2d, ya, yb, yc, w_proj, wa, wb, wc, wo, x2d, g, l, tm, tn):
    t, d = x2d.shape
    width = ya.shape[-1]
    n_n = d // tn
    tile = pl.BlockSpec((tm, d), lambda i, j: (i, 0))
    y_spec = pl.BlockSpec((tm, width), lambda i, j: (i, 0))
    gate = lambda k: pl.BlockSpec(
        (None, d, tn), lambda i, j: (l, 0, (MERGE_COL + k * d) // tn + j))
    w_spec = pl.BlockSpec((None, width, tn), lambda i, j: (l, 0, j))
    return pl.pallas_call(
        functools.partial(_merge_kernel, n_n=n_n),
        grid=(t // tm, n_n),
        in_specs=[
            tile, y_spec, y_spec, y_spec, gate(0), gate(1), gate(2), w_spec, w_spec, w_spec,
            pl.BlockSpec((None, tn, d), lambda i, j: (l, j, 0)),
            tile,
            pl.BlockSpec((None, 1, d), lambda i, j: (l, 0, 0)),
        ],
        out_specs=tile,
        out_shape=jax.ShapeDtypeStruct((t, d), F32),
        compiler_params=pltpu.CompilerParams(
            dimension_semantics=("parallel", "arbitrary"), vmem_limit_bytes=VMEM_LIMIT_MERGE),
        name="merge",
    )(h2d, ya, yb, yc, w_proj, w_proj, w_proj, wa, wb, wc, wo, x2d, g)


def _tile(n, want):
    t = min(n, want)
    assert n % t == 0, (n, t)
    return t


def kernel(x, ffn1_pre_g, ffn1_w_gate, ffn1_w_up, ffn1_w_down, ffn1_post_g, mix_pre_g, w_in, conv_w, conv_b, lru_w_a, lru_b_a, lru_w_x, lru_b_x, lru_lambda, gla_w2, gla_b2, gla_norm_g, hgrn_lb_logits, hgrn_norm_g, w_br_a, w_br_b, w_br_c, w_out, mix_post_g, ffn2_pre_g, ffn2_w_gate, ffn2_w_up, ffn2_w_down, ffn2_post_g):
    bsz, seq, d = x.shape
    depth = w_in.shape[0]
    t = bsz * seq
    assert d == D_MODEL and seq % CHUNK == 0

    bf = lambda w: w.astype(BF16)
    row = lambda v: v.reshape(depth, 1, v.shape[-1])
    w_in_bf = bf(w_in)
    w_proj = jnp.concatenate(
        [w_in_bf[:, :, :LR_COL + B_RANK],
         jnp.zeros((depth, d, LR_PAD - B_RANK), BF16),
         w_in_bf[:, :, LR_COL + B_RANK:]], axis=-1)
    w2 = jnp.concatenate(
        [bf(gla_w2), jnp.zeros((depth, 128 - B_RANK, gla_w2.shape[-1]), BF16)], axis=1)
    ffn1 = (row(ffn1_pre_g), bf(ffn1_w_gate), bf(ffn1_w_up), bf(ffn1_w_down), row(ffn1_post_g))
    ffn2 = (row(ffn2_pre_g), bf(ffn2_w_gate), bf(ffn2_w_up), bf(ffn2_w_down), row(ffn2_post_g))
    lru = (conv_w, row(conv_b), bf(lru_w_a), row(lru_b_a), bf(lru_w_x), row(lru_b_x),
           row(lru_lambda))
    br = (bf(w_br_a), bf(w_br_b), bf(w_br_c), bf(w_out))

    tm_ffn, tf = _tile(t, 512), _tile(D_FF, 512)
    tm_proj, tn_proj = _tile(t, 2048), 512
    tm_merge, tn_merge = _tile(t, 512), 512
    tt_lru, tt_gla = _tile(seq, 256), _tile(seq, 1024)

    x2 = x.reshape(t, d)
    for l in range(depth):
        x2, h2 = _ffn(x2, *ffn1, row(mix_pre_g), l, tm_ffn, tf)
        proj3 = _proj(h2, w_proj, l, MERGE_COL, tm_proj, tn_proj).reshape(bsz, seq, MERGE_COL)
        y_a = _rglru(proj3, *lru, l, tt_lru)
        y_b = _gla_b(proj3, w2, row(gla_b2), row(gla_norm_g), l, tt_gla)
        y_c = _gla_c(proj3, hgrn_lb_logits, row(hgrn_norm_g), l, tt_gla)
        x2 = _merge(h2, y_a.reshape(t, -1), y_b.reshape(t, -1), y_c.reshape(t, -1), w_proj,
                    *br, x2, row(mix_post_g), l, tm_merge, tn_merge)
        x2 = _ffn(x2, *ffn2, None, l, tm_ffn, tf)
    return x2.reshape(bsz, seq, d)
```

```python
import functools

import numpy as np
import jax
import jax.numpy as jnp
from jax import lax
from jax.experimental import pallas as pl
from jax.experimental.pallas import tpu as pltpu

F32 = jnp.float32
BF16 = jnp.bfloat16

D_MODEL = 2048
D_FF = 5632
A_WIDTH = 1024
A_HEADS = 8
A_HEAD_DIM = 128
CONV_WIDTH = 4
LRU_C = 8.0
B_HEADS = 4
B_DK = 128
B_DV = 256
B_RANK = 16
GATE_NORMALIZER = 16.0
C_HEADS = 8
C_DK = 128
C_DV = 128
EPS = 1e-6
EXP_CLIP = 80.0
LOG2E = 1.4426950408889634

LR_PAD = 512
LR_COL = 5120
PROJ_WIDTH = 15376 - B_RANK + LR_PAD
C_COL = LR_COL + LR_PAD
MERGE_COL = C_COL + 4096

CHUNK = 64
SUB = 8
N_SUB = CHUNK // SUB
HEADS_PER_STEP = 2
SCAN_GROUP = 8

VMEM_LIMIT = 48 * 1024 * 1024
VMEM_LIMIT_MERGE = 58 * 1024 * 1024


def _rms(x, g):
    return x * lax.rsqrt(jnp.mean(x * x, axis=-1, keepdims=True) + EPS) * g


def _sigmoid(x):
    return 0.5 * jnp.tanh(0.5 * x) + 0.5


def _softplus(x):
    return jnp.maximum(x, 0.0) + jnp.log1p(jnp.exp(-jnp.abs(x)))


def _log_sigmoid(x):
    return -_softplus(-x)


def _gelu_tanh(x):
    return 0.5 * x * (1.0 + jnp.tanh(np.sqrt(2.0 / np.pi) * (x + 0.044715 * (x * x * x))))


def _dot(a, b):
    return jnp.dot(a, b, preferred_element_type=F32)


def _dot_nt(a, b):
    return lax.dot_general(a, b, (((1,), (1,)), ((), ())), preferred_element_type=F32)


def _dot_tn(a, b):
    return lax.dot_general(a, b, (((0,), (0,)), ((), ())), preferred_element_type=F32)


def _ffn_kernel(x_ref, pre_g_ref, wg_ref, wu_ref, wd_ref, post_g_ref, *rest, n_f, emit_norm):
    if emit_norm:
        next_g_ref, o_ref, hn_ref, h_ref = rest
    else:
        o_ref, h_ref = rest
    j = pl.program_id(1)

    @pl.when(j == 0)
    def _():
        h_ref[...] = _rms(x_ref[...], pre_g_ref[...]).astype(BF16)
        o_ref[...] = jnp.zeros_like(o_ref)

    h = h_ref[...]
    tf = wg_ref.shape[1]
    acc = None
    for c0 in range(0, tf, tf // 2):
        cols = slice(c0, c0 + tf // 2)
        g = _dot(h, wg_ref[:, cols])
        u = _dot(h, wu_ref[:, cols])
        act = (g * _sigmoid(g) * u).astype(BF16)
        part = _dot(act, wd_ref[cols, :])
        acc = part if acc is None else acc + part
    o_ref[...] += acc

    @pl.when(j == n_f - 1)
    def _():
        y = x_ref[...] + _rms(o_ref[...], 0.5 * post_g_ref[...])
        o_ref[...] = y
        if emit_norm:
            hn_ref[...] = _rms(y, next_g_ref[...]).astype(BF16)


def _ffn(x2d, pre_g, wg, wu, wd, post_g, next_g, l, tm, tf):
    t, d = x2d.shape
    f = wg.shape[-1]
    emit_norm = next_g is not None
    gain = pl.BlockSpec((None, 1, d), lambda i, j: (l, 0, 0))
    tile = pl.BlockSpec((tm, d), lambda i, j: (i, 0))
    return pl.pallas_call(
        functools.partial(_ffn_kernel, n_f=f // tf, emit_norm=emit_norm),
        grid=(t // tm, f // tf),
        in_specs=[
            tile, gain,
            pl.BlockSpec((None, d, tf), lambda i, j: (l, 0, j)),
            pl.BlockSpec((None, d, tf), lambda i, j: (l, 0, j)),
            pl.BlockSpec((None, tf, d), lambda i, j: (l, j, 0)),
            gain,
        ] + ([gain] if emit_norm else []),
        out_specs=(tile, tile) if emit_norm else tile,
        out_shape=((jax.ShapeDtypeStruct((t, d), F32), jax.ShapeDtypeStruct((t, d), BF16))
                   if emit_norm else jax.ShapeDtypeStruct((t, d), F32)),
        scratch_shapes=[pltpu.VMEM((tm, d), BF16)],
        compiler_params=pltpu.CompilerParams(
            dimension_semantics=("parallel", "arbitrary"), vmem_limit_bytes=VMEM_LIMIT),
        name="ffn",
    )(x2d, pre_g, wg, wu, wd, post_g, *((next_g,) if emit_norm else ()))


def _proj_kernel(h_ref, w_ref, o_ref):
    o_ref[...] = _dot(h_ref[...], w_ref[...])


def _proj(h2d, w, l, n, tm, tn):
    t, d = h2d.shape
    return pl.pallas_call(
        _proj_kernel,
        grid=(t // tm, n // tn),
        in_specs=[
            pl.BlockSpec((tm, d), lambda i, j: (i, 0)),
            pl.BlockSpec((None, d, tn), lambda i, j: (l, 0, j)),
        ],
        out_specs=pl.BlockSpec((tm, tn), lambda i, j: (i, j)),
        out_shape=jax.ShapeDtypeStruct((t, n), F32),
        compiler_params=pltpu.CompilerParams(
            dimension_semantics=("parallel", "arbitrary"), vmem_limit_bytes=VMEM_LIMIT),
        name="proj",
    )(h2d, w)


def _rglru_kernel(ax_ref, ag_ref, cw_ref, cb_ref, wa_ref, ba_ref, wx_ref, bx_ref, lam_ref,
                  o_ref, xs_ref, h_ref, *, tt):
    halo = 8

    @pl.when(pl.program_id(1) == 0)
    def _():
        xs_ref[0:halo, :] = jnp.zeros((halo, A_WIDTH), F32)
        h_ref[...] = jnp.zeros_like(h_ref)

    xa = ax_ref[...]
    xs_ref[halo:halo + tt, :] = xa
    cw = cw_ref[...]
    conv = cb_ref[...] + cw[3:4, :] * xa
    for k in range(CONV_WIDTH - 1):
        back = CONV_WIDTH - 1 - k
        conv = conv + cw[k:k + 1, :] * xs_ref[halo - back:halo - back + tt, :]
    xs_ref[0:halo, :] = xs_ref[tt:tt + halo, :]

    conv_bf = conv.astype(BF16)
    r_parts, i_parts = [], []
    for hd in range(A_HEADS):
        xh = conv_bf[:, hd * A_HEAD_DIM:(hd + 1) * A_HEAD_DIM]
        r_parts.append(_dot(xh, wa_ref[hd]))
        i_parts.append(_dot(xh, wx_ref[hd]))
    r = _sigmoid(jnp.concatenate(r_parts, axis=1) + ba_ref[...])
    i = _sigmoid(jnp.concatenate(i_parts, axis=1) + bx_ref[...])
    log_a = (-LRU_C * _softplus(-lam_ref[...])) * r
    a = jnp.exp(log_a)
    th = jnp.tanh(log_a)
    u = jnp.sqrt(-2.0 * th / (1.0 - th)) * (i * conv)

    a = a.reshape(tt // SCAN_GROUP, SCAN_GROUP, A_WIDTH)
    u = u.reshape(tt // SCAN_GROUP, SCAN_GROUP, A_WIDTH)
    row_in_group = lax.broadcasted_iota(jnp.int32, a.shape, 1)
    s = 1
    while s < SCAN_GROUP:
        keep = row_in_group >= s
        u = a * jnp.where(keep, pltpu.roll(u, s, 1), 0.0) + u
        a = a * jnp.where(keep, pltpu.roll(a, s, 1), 1.0)
        s *= 2
    carry = h_ref[...]
    groups = []
    for g in range(tt // SCAN_GROUP):
        hg = u[g] + a[g] * carry
        carry = hg[SCAN_GROUP - 1:SCAN_GROUP, :]
        groups.append(hg)
    h_ref[...] = carry
    o_ref[...] = (_gelu_tanh(ag_ref[...]) * jnp.concatenate(groups, axis=0)).astype(BF16)


def _rglru(proj3, conv_w, conv_b, wa, ba, wx, bx, lam, l, tt):
    b, s, _ = proj3.shape
    vec = pl.BlockSpec((None, 1, A_WIDTH), lambda bi, ti: (l, 0, 0))
    blk = pl.BlockSpec((None, A_HEADS, A_HEAD_DIM, A_HEAD_DIM), lambda bi, ti: (l, 0, 0, 0))
    return pl.pallas_call(
        functools.partial(_rglru_kernel, tt=tt),
        grid=(b, s // tt),
        in_specs=[
            pl.BlockSpec((None, tt, A_WIDTH), lambda bi, ti: (bi, ti, 0)),
            pl.BlockSpec((None, tt, A_WIDTH), lambda bi, ti: (bi, ti, 1)),
            pl.BlockSpec((None, CONV_WIDTH, A_WIDTH), lambda bi, ti: (l, 0, 0)),
            vec, blk, vec, blk, vec, vec,
        ],
        out_specs=pl.BlockSpec((None, tt, A_WIDTH), lambda bi, ti: (bi, ti, 0)),
        out_shape=jax.ShapeDtypeStruct((b, s, A_WIDTH), BF16),
        scratch_shapes=[pltpu.VMEM((tt + 8, A_WIDTH), F32), pltpu.VMEM((1, A_WIDTH), F32)],
        compiler_params=pltpu.CompilerParams(
            dimension_semantics=("parallel", "arbitrary"), vmem_limit_bytes=VMEM_LIMIT),
        name="rglru",
    )(proj3, proj3, conv_w, conv_b, wa, ba, wx, bx, lam)


def _block_row_bcast(ref, r0, j, lanes):
    d = lanes.stop - lanes.start
    return jnp.concatenate(
        [jnp.broadcast_to(ref[r0 + SUB * blk + j:r0 + SUB * blk + j + 1, lanes], (SUB, d))
         for blk in range(N_SUB)], axis=0)


def _cumulative_decay(g_s, b_s, tt, tri):
    for r0 in range(0, tt, CHUNK):
        rows = slice(r0, r0 + CHUNK)
        g2 = g_s[rows, :]
        g_hi = g2.astype(BF16)
        g_r = g2 - g_hi.astype(F32)
        g_mid = g_r.astype(BF16)
        g_lo = (g_r - g_mid.astype(F32)).astype(BF16)
        b_s[rows, :] = _dot(tri, g_hi) + _dot(tri, g_mid) + _dot(tri, g_lo)


def _gla_state_update(k_s, b_s, r0, lanes, v):
    rows = slice(r0, r0 + CHUNK)
    k, b = k_s[rows, lanes], b_s[rows, lanes]
    khat = k * jnp.exp2(b[CHUNK - 1:CHUNK, :] - b)
    return _dot_tn(v.astype(BF16), khat.astype(BF16))


def _gla_chunk(q_s, k_s, b_s, r0, lanes, v, st_bf, diag_mask, level_masks):
    rows = slice(r0, r0 + CHUNK)
    q, k, b = q_s[rows, lanes], k_s[rows, lanes], b_s[rows, lanes]
    dk = q.shape[1]

    o = _dot_nt((q * jnp.exp2(b)).astype(BF16), st_bf)

    col_in_blk = lax.broadcasted_iota(jnp.int32, (CHUNK, CHUNK), 1) % SUB
    scores = jnp.zeros((CHUNK, CHUNK), F32)
    for j in range(SUB):
        dec = jnp.exp2(b - _block_row_bcast(b_s, r0, j, lanes))
        s_j = jnp.sum(q * _block_row_bcast(k_s, r0, j, lanes) * dec, axis=1, keepdims=True)
        scores = jnp.where(col_in_blk == j, s_j, scores)
    scores = jnp.where(diag_mask, scores, 0.0)

    size = SUB
    for mask in level_masks:
        q_rows, k_rows = [], []
        zeros = jnp.zeros((size, dk), F32)
        for lo in range(0, CHUNK, 2 * size):
            mid, hi = lo + size, lo + 2 * size
            r = b[mid - 1:mid, :]
            k_rows += [k[lo:mid, :] * jnp.exp2(r - b[lo:mid, :]), zeros]
            q_rows += [zeros, q[mid:hi, :] * jnp.exp2(b[mid:hi, :] - r)]
        s_lvl = _dot_nt(jnp.concatenate(q_rows, axis=0).astype(BF16),
                        jnp.concatenate(k_rows, axis=0).astype(BF16))
        scores = scores + (s_lvl if mask is None else jnp.where(mask, s_lvl, 0.0))
        size *= 2
    return o + _dot(scores.astype(BF16), v.astype(BF16))


def _chunk_constants():
    row = lax.broadcasted_iota(jnp.int32, (CHUNK, CHUNK), 0)
    col = lax.broadcasted_iota(jnp.int32, (CHUNK, CHUNK), 1)
    tri = jnp.where(row >= col, 1.0, 0.0).astype(BF16)
    diag_mask = (row // SUB == col // SUB) & (col <= row)
    level_masks = []
    size = SUB
    while 2 * size < CHUNK:
        level_masks.append(row // (2 * size) == col // (2 * size))
        size *= 2
    level_masks.append(None)
    return tri, diag_mask, level_masks


def _gla_scratch(tt, dk, dv):
    hp, n_chunks = HEADS_PER_STEP, tt // CHUNK
    return ([pltpu.VMEM((tt, hp * dk), F32)] * 4
            + [pltpu.VMEM((hp, n_chunks, dv, dk), F32), pltpu.VMEM((hp, n_chunks, dv, dk), BF16),
               pltpu.VMEM((hp, dv, dk), F32)])


def _gla_loop(q_s, k_s, g_s, b_s, u_s, sb_s, v_ref, gate_ref, gate_fn, norm_g, st_ref, o_ref,
              tt, dk, dv):
    tri, diag_mask, level_masks = _chunk_constants()
    heads = [(h, slice(h * dk, (h + 1) * dk), slice(h * dv, (h + 1) * dv))
             for h in range(HEADS_PER_STEP)]
    chunks = list(enumerate(range(0, tt, CHUNK)))

    _cumulative_decay(g_s, b_s, tt, tri)
    for c, r0 in chunks:
        for h, kl, vl in heads:
            u_s[h, c] = _gla_state_update(k_s, b_s, r0, kl, v_ref[r0:r0 + CHUNK, vl])
    for h, kl, vl in heads:
        st = st_ref[h]
        for c, r0 in chunks:
            sb_s[h, c] = st.astype(BF16)
            st = st * jnp.exp2(b_s[r0 + CHUNK - 1:r0 + CHUNK, kl]) + u_s[h, c]
        st_ref[h] = st
    for c, r0 in chunks:
        rows = slice(r0, r0 + CHUNK)
        for h, kl, vl in heads:
            o = _gla_chunk(q_s, k_s, b_s, r0, kl, v_ref[rows, vl], sb_s[h, c],
                           diag_mask, level_masks)
            y = o * lax.rsqrt(jnp.mean(o * o, axis=-1, keepdims=True) + EPS) * norm_g
            o_ref[rows, vl] = (y * gate_fn(gate_ref[rows, vl])).astype(BF16)


def _silu(x):
    return x * _sigmoid(x)


def _gla_b_kernel(q_ref, k_ref, v_ref, gate_ref, lr_ref, w2_ref, b2_ref, ng_ref,
                  o_ref, q_s, k_s, g_s, b_s, u_s, sb_s, st_ref, *, tt):
    @pl.when(pl.program_id(2) == 0)
    def _():
        st_ref[...] = jnp.zeros_like(st_ref)

    q_s[...] = q_ref[...] * (B_DK ** -0.5)
    k_s[...] = k_ref[...]
    z = _dot(lr_ref[...].astype(BF16), w2_ref[...]) + b2_ref[...]
    g_s[...] = _log_sigmoid(z) * (LOG2E / GATE_NORMALIZER)
    _gla_loop(q_s, k_s, g_s, b_s, u_s, sb_s, v_ref, gate_ref, _silu, ng_ref[...], st_ref, o_ref,
              tt, B_DK, B_DV)


def _gla_b(proj3, w2, b2, norm_g, l, tt):
    b, s, _ = proj3.shape
    hp = HEADS_PER_STEP
    tok = lambda width, base: pl.BlockSpec(
        (None, tt, hp * width), lambda bi, h, ti: (bi, ti, base // (hp * width) + h))
    return pl.pallas_call(
        functools.partial(_gla_b_kernel, tt=tt),
        grid=(b, B_HEADS // hp, s // tt),
        in_specs=[
            tok(B_DK, 2048), tok(B_DK, 2560), tok(B_DV, 3072), tok(B_DV, 4096),
            pl.BlockSpec((None, tt, 128), lambda bi, h, ti: (bi, ti, LR_COL // 128)),
            pl.BlockSpec((None, 128, hp * B_DK), lambda bi, h, ti: (l, 0, h)),
            pl.BlockSpec((None, 1, hp * B_DK), lambda bi, h, ti: (l, 0, h)),
            pl.BlockSpec((None, 1, B_DV), lambda bi, h, ti: (l, 0, 0)),
        ],
        out_specs=pl.BlockSpec((None, tt, hp * B_DV), lambda bi, h, ti: (bi, ti, h)),
        out_shape=jax.ShapeDtypeStruct((b, s, B_HEADS * B_DV), BF16),
        scratch_shapes=_gla_scratch(tt, B_DK, B_DV),
        compiler_params=pltpu.CompilerParams(
            dimension_semantics=("parallel", "parallel", "arbitrary"),
            vmem_limit_bytes=VMEM_LIMIT),
        name="gla_b",
    )(proj3, proj3, proj3, proj3, proj3, w2, b2, norm_g)


def _gla_c_kernel(q_ref, f_ref, v_ref, gate_ref, lbl_ref, ng_ref,
                  o_ref, q_s, k_s, g_s, b_s, u_s, sb_s, st_ref, *, tt, layer):
    @pl.when(pl.program_id(2) == 0)
    def _():
        st_ref[...] = jnp.zeros_like(st_ref)

    logits = lbl_ref[...]
    p = jnp.exp(logits - jnp.max(logits, axis=0, keepdims=True))
    p = p / jnp.sum(p, axis=0, keepdims=True)
    lb = jnp.zeros((1, logits.shape[1]), F32)
    for m in range(1, layer + 1):
        lb = lb + p[m:m + 1, :]

    z = f_ref[...]
    q_s[...] = _silu(q_ref[...])
    t = jnp.exp(jnp.minimum(-z, EXP_CLIP))
    r = 1.0 / (1.0 + t)
    g_s[...] = (jnp.log((1.0 + lb * t) * r) + jnp.minimum(z + EXP_CLIP, 0.0)) * LOG2E
    k_s[...] = (1.0 - lb) * (t * r)
    _gla_loop(q_s, k_s, g_s, b_s, u_s, sb_s, v_ref, gate_ref, _sigmoid, ng_ref[...], st_ref, o_ref,
              tt, C_DK, C_DV)


def _gla_c(proj3, lb_logits, norm_g, l, tt):
    b, s, _ = proj3.shape
    depth = lb_logits.shape[0]
    hp = HEADS_PER_STEP
    tok = lambda base: pl.BlockSpec(
        (None, tt, hp * C_DK), lambda bi, h, ti: (bi, ti, base // (hp * C_DK) + h))
    return pl.pallas_call(
        functools.partial(_gla_c_kernel, tt=tt, layer=l),
        grid=(b, C_HEADS // hp, s // tt),
        in_specs=[
            tok(C_COL), tok(C_COL + 1024), tok(C_COL + 2048), tok(C_COL + 3072),
            pl.BlockSpec((depth, hp * C_DK), lambda bi, h, ti: (0, h)),
            pl.BlockSpec((None, 1, C_DV), lambda bi, h, ti: (l, 0, 0)),
        ],
        out_specs=pl.BlockSpec((None, tt, hp * C_DV), lambda bi, h, ti: (bi, ti, h)),
        out_shape=jax.ShapeDtypeStruct((b, s, C_HEADS * C_DV), BF16),
        scratch_shapes=_gla_scratch(tt, C_DK, C_DV),
        compiler_params=pltpu.CompilerParams(
            dimension_semantics=("parallel", "parallel", "arbitrary"),
            vmem_limit_bytes=VMEM_LIMIT),
        name="gla_c",
    )(proj3, proj3, proj3, proj3, lb_logits, norm_g)


def _merge_kernel(h_ref, ya_ref, yb_ref, yc_ref, ga_ref, gb_ref, gc_ref, wa_ref, wb_ref, wc_ref,
                  wo_ref, x_ref, g_ref, o_ref, *, n_n):
    j = pl.program_id(1)

    @pl.when(j == 0)
    def _():
        o_ref[...] = jnp.zeros_like(o_ref)

    h = h_ref[...]
    merged = (_sigmoid(_dot(h, ga_ref[...])) * _dot(ya_ref[...], wa_ref[...])
              + _sigmoid(_dot(h, gb_ref[...])) * _dot(yb_ref[...], wb_ref[...])
              + _sigmoid(_dot(h, gc_ref[...])) * _dot(yc_ref[...], wc_ref[...]))
    o_ref[...] += _dot(merged.astype(BF16), wo_ref[...])

    @pl.when(j == n_n - 1)
    def _():
        o_ref[...] = x_ref[...] + _rms(o_ref[...], g_ref[...])


def _merge(h2d, ya, yb, yc, w_proj, wa, wb, wc, wo, x2d, g, l, tm, tn):
    t, d = x2d.shape
    width = ya.shape[-1]
    n_n = d // tn
    tile = pl.BlockSpec((tm, d), lambda i, j: (i, 0))
    y_spec = pl.BlockSpec((tm, width), lambda i, j: (i, 0))
    gate = lambda k: pl.BlockSpec(
        (None, d, tn), lambda i, j: (l, 0, (MERGE_COL + k * d) // tn + j))
    w_spec = pl.BlockSpec((None, width, tn), lambda i, j: (l, 0, j))
    return pl.pallas_call(
        functools.partial(_merge_kernel, n_n=n_n),
        grid=(t // tm, n_n),
        in_specs=[
            tile, y_spec, y_spec, y_spec, gate(0), gate(1), gate(2), w_spec, w_spec, w_spec,
            pl.BlockSpec((None, tn, d), lambda i, j: (l, j, 0)),
            tile,
            pl.BlockSpec((None, 1, d), lambda i, j: (l, 0, 0)),
        ],
        out_specs=tile,
        out_shape=jax.ShapeDtypeStruct((t, d), F32),
        compiler_params=pltpu.CompilerParams(
            dimension_semantics=("parallel", "arbitrary"), vmem_limit_bytes=VMEM_LIMIT_MERGE),
        name="merge",
    )(h2d, ya, yb, yc, w_proj, w_proj, w_proj, wa, wb, wc, wo, x2d, g)


def _relayout_kernel(w_ref, o_ref):
    lr_end = LR_COL + B_RANK
    rows = w_ref.shape[0]
    o_ref[:, :lr_end] = w_ref[:, :lr_end].astype(BF16)
    o_ref[:, lr_end:C_COL] = jnp.zeros((rows, C_COL - lr_end), BF16)
    o_ref[:, C_COL:] = w_ref[:, lr_end:].astype(BF16)


def _relayout_w_in(w_in, rows):
    depth, d, n = w_in.shape
    return pl.pallas_call(
        _relayout_kernel,
        grid=(depth, d // rows),
        in_specs=[pl.BlockSpec((None, rows, n), lambda l, i: (l, i, 0))],
        out_specs=pl.BlockSpec((None, rows, PROJ_WIDTH), lambda l, i: (l, i, 0)),
        out_shape=jax.ShapeDtypeStruct((depth, d, PROJ_WIDTH), BF16),
        compiler_params=pltpu.CompilerParams(
            dimension_semantics=("parallel", "parallel"), vmem_limit_bytes=VMEM_LIMIT),
        name="relayout",
    )(w_in)


def _tile(n, want):
    t = min(n, want)
    assert n % t == 0, (n, t)
    return t


def kernel(x, ffn1_pre_g, ffn1_w_gate, ffn1_w_up, ffn1_w_down, ffn1_post_g, mix_pre_g, w_in, conv_w, conv_b, lru_w_a, lru_b_a, lru_w_x, lru_b_x, lru_lambda, gla_w2, gla_b2, gla_norm_g, hgrn_lb_logits, hgrn_norm_g, w_br_a, w_br_b, w_br_c, w_out, mix_post_g, ffn2_pre_g, ffn2_w_gate, ffn2_w_up, ffn2_w_down, ffn2_post_g):
    bsz, seq, d = x.shape
    depth = w_in.shape[0]
    t = bsz * seq
    assert d == D_MODEL and seq % CHUNK == 0

    bf = lambda w: w.astype(BF16)
    row = lambda v: v.reshape(depth, 1, v.shape[-1])
    w_proj = _relayout_w_in(w_in, 128)
    w2 = jnp.concatenate(
        [bf(gla_w2), jnp.zeros((depth, 128 - B_RANK, gla_w2.shape[-1]), BF16)], axis=1)
    ffn1 = (row(ffn1_pre_g), bf(ffn1_w_gate), bf(ffn1_w_up), bf(ffn1_w_down), row(ffn1_post_g))
    ffn2 = (row(ffn2_pre_g), bf(ffn2_w_gate), bf(ffn2_w_up), bf(ffn2_w_down), row(ffn2_post_g))
    lru = (conv_w, row(conv_b), bf(lru_w_a), row(lru_b_a), bf(lru_w_x), row(lru_b_x),
           row(lru_lambda))
    br = (bf(w_br_a), bf(w_br_b), bf(w_br_c), bf(w_out))

    tm_ffn, tf = _tile(t, 512), _tile(D_FF, 512)
    tm_proj, tn_proj = _tile(t, 2048), 512
    tm_merge, tn_merge = _tile(t, 512), 512
    tt_lru, tt_gla = _tile(seq, 256), _tile(seq, 1024)

    x2 = x.reshape(t, d)
    for l in range(depth):
        x2, h2 = _ffn(x2, *ffn1, row(mix_pre_g), l, tm_ffn, tf)
        proj3 = _proj(h2, w_proj, l, MERGE_COL, tm_proj, tn_proj).reshape(bsz, seq, MERGE_COL)
        y_a = _rglru(proj3, *lru, l, tt_lru)
        y_b = _gla_b(proj3, w2, row(gla_b2), row(gla_norm_g), l, tt_gla)
        y_c = _gla_c(proj3, hgrn_lb_logits, row(hgrn_norm_g), l, tt_gla)
        x2 = _merge(h2, y_a.reshape(t, -1), y_b.reshape(t, -1), y_c.reshape(t, -1), w_proj,
                    *br, x2, row(mix_post_g), l, tm_merge, tn_merge)
        x2 = _ffn(x2, *ffn2, None, l, tm_ffn, tf)
    return x2.reshape(bsz, seq, d)
```

```python
import functools

import numpy as np
import jax
import jax.numpy as jnp
from jax import lax
from jax.experimental import pallas as pl
from jax.experimental.pallas import tpu as pltpu

F32 = jnp.float32
BF16 = jnp.bfloat16

D_MODEL = 2048
D_FF = 5632
A_WIDTH = 1024
A_HEADS = 8
A_HEAD_DIM = 128
CONV_WIDTH = 4
LRU_C = 8.0
B_HEADS = 4
B_DK = 128
B_DV = 256
B_RANK = 16
GATE_NORMALIZER = 16.0
C_HEADS = 8
C_DK = 128
C_DV = 128
EPS = 1e-6
EXP_CLIP = 80.0
LOG2E = 1.4426950408889634

LR_PAD = 512
LR_COL = 5120
PROJ_WIDTH = 15376 - B_RANK + LR_PAD
C_COL = LR_COL + LR_PAD
MERGE_COL = C_COL + 4096

CHUNK = 64
SUB = 8
N_SUB = CHUNK // SUB
HEADS_PER_STEP = 2
SCAN_GROUP = 8

VMEM_LIMIT = 48 * 1024 * 1024
VMEM_LIMIT_MERGE = 58 * 1024 * 1024


def _rms(x, g):
    return x * lax.rsqrt(jnp.mean(x * x, axis=-1, keepdims=True) + EPS) * g


def _sigmoid(x):
    return 0.5 * jnp.tanh(0.5 * x) + 0.5


def _softplus(x):
    return jnp.maximum(x, 0.0) + jnp.log1p(jnp.exp(-jnp.abs(x)))


def _log_sigmoid(x):
    return -_softplus(-x)


def _gelu_tanh(x):
    return 0.5 * x * (1.0 + jnp.tanh(np.sqrt(2.0 / np.pi) * (x + 0.044715 * (x * x * x))))


def _dot(a, b):
    return jnp.dot(a, b, preferred_element_type=F32)


def _dot_nt(a, b):
    return lax.dot_general(a, b, (((1,), (1,)), ((), ())), preferred_element_type=F32)


def _dot_tn(a, b):
    return lax.dot_general(a, b, (((0,), (0,)), ((), ())), preferred_element_type=F32)


def _ffn_kernel(x_ref, pre_ref, wg_ref, wu_ref, wd_ref, post_g_ref, *rest, n_f, h_given,
                emit_norm):
    rest = list(rest)
    next_g_ref = rest.pop(0) if emit_norm else None
    o_ref = rest.pop(0)
    hn_ref = rest.pop(0) if emit_norm else None
    h_ref = pre_ref if h_given else rest.pop(0)
    j = pl.program_id(1)

    @pl.when(j == 0)
    def _():
        if not h_given:
            h_ref[...] = _rms(x_ref[...], pre_ref[...]).astype(BF16)
        o_ref[...] = jnp.zeros_like(o_ref)

    h = h_ref[...]
    tf = wg_ref.shape[1]
    acc = None
    for c0 in range(0, tf, tf // 2):
        cols = slice(c0, c0 + tf // 2)
        g = _dot(h, wg_ref[:, cols])
        u = _dot(h, wu_ref[:, cols])
        act = (g * _sigmoid(g) * u).astype(BF16)
        part = _dot(act, wd_ref[cols, :])
        acc = part if acc is None else acc + part
    o_ref[...] += acc

    @pl.when(j == n_f - 1)
    def _():
        y = x_ref[...] + _rms(o_ref[...], 0.5 * post_g_ref[...])
        o_ref[...] = y
        if emit_norm:
            hn_ref[...] = _rms(y, next_g_ref[...]).astype(BF16)


def _gain_spec(layer, d):
    return pl.BlockSpec((None, 1, d), lambda i, j: (layer, 0, 0))


def _ffn(x2d, h2d, pre_g, wg, wu, wd, post_g, next_g, next_l, l, tm, tf):
    t, d = x2d.shape
    f = wg.shape[-1]
    h_given, emit_norm = h2d is not None, next_g is not None
    tile = pl.BlockSpec((tm, d), lambda i, j: (i, 0))
    return pl.pallas_call(
        functools.partial(_ffn_kernel, n_f=f // tf, h_given=h_given, emit_norm=emit_norm),
        grid=(t // tm, f // tf),
        in_specs=[
            tile, tile if h_given else _gain_spec(l, d),
            pl.BlockSpec((None, d, tf), lambda i, j: (l, 0, j)),
            pl.BlockSpec((None, d, tf), lambda i, j: (l, 0, j)),
            pl.BlockSpec((None, tf, d), lambda i, j: (l, j, 0)),
            _gain_spec(l, d),
        ] + ([_gain_spec(next_l, d)] if emit_norm else []),
        out_specs=(tile, tile) if emit_norm else tile,
        out_shape=((jax.ShapeDtypeStruct((t, d), F32), jax.ShapeDtypeStruct((t, d), BF16))
                   if emit_norm else jax.ShapeDtypeStruct((t, d), F32)),
        scratch_shapes=[] if h_given else [pltpu.VMEM((tm, d), BF16)],
        compiler_params=pltpu.CompilerParams(
            dimension_semantics=("parallel", "arbitrary"), vmem_limit_bytes=VMEM_LIMIT),
        name="ffn",
    )(x2d, h2d if h_given else pre_g, wg, wu, wd, post_g, *((next_g,) if emit_norm else ()))


def _proj_kernel(h_ref, w_ref, o_ref):
    o_ref[...] = _dot(h_ref[...], w_ref[...])


def _proj(h2d, w, l, n, tm, tn):
    t, d = h2d.shape
    return pl.pallas_call(
        _proj_kernel,
        grid=(t // tm, n // tn),
        in_specs=[
            pl.BlockSpec((tm, d), lambda i, j: (i, 0)),
            pl.BlockSpec((None, d, tn), lambda i, j: (l, 0, j)),
        ],
        out_specs=pl.BlockSpec((tm, tn), lambda i, j: (i, j)),
        out_shape=jax.ShapeDtypeStruct((t, n), F32),
        compiler_params=pltpu.CompilerParams(
            dimension_semantics=("parallel", "arbitrary"), vmem_limit_bytes=VMEM_LIMIT),
        name="proj",
    )(h2d, w)


def _rglru_kernel(ax_ref, ag_ref, cw_ref, cb_ref, wa_ref, ba_ref, wx_ref, bx_ref, lam_ref,
                  o_ref, xs_ref, h_ref, *, tt):
    halo = 8

    @pl.when(pl.program_id(1) == 0)
    def _():
        xs_ref[0:halo, :] = jnp.zeros((halo, A_WIDTH), F32)
        h_ref[...] = jnp.zeros_like(h_ref)

    xa = ax_ref[...]
    xs_ref[halo:halo + tt, :] = xa
    cw = cw_ref[...]
    conv = cb_ref[...] + cw[3:4, :] * xa
    for k in range(CONV_WIDTH - 1):
        back = CONV_WIDTH - 1 - k
        conv = conv + cw[k:k + 1, :] * xs_ref[halo - back:halo - back + tt, :]
    xs_ref[0:halo, :] = xs_ref[tt:tt + halo, :]

    conv_bf = conv.astype(BF16)
    r_parts, i_parts = [], []
    for hd in range(A_HEADS):
        xh = conv_bf[:, hd * A_HEAD_DIM:(hd + 1) * A_HEAD_DIM]
        r_parts.append(_dot(xh, wa_ref[hd]))
        i_parts.append(_dot(xh, wx_ref[hd]))
    r = _sigmoid(jnp.concatenate(r_parts, axis=1) + ba_ref[...])
    i = _sigmoid(jnp.concatenate(i_parts, axis=1) + bx_ref[...])
    log_a = (-LRU_C * _softplus(-lam_ref[...])) * r
    a = jnp.exp(log_a)
    th = jnp.tanh(log_a)
    u = jnp.sqrt(-2.0 * th / (1.0 - th)) * (i * conv)

    a = a.reshape(tt // SCAN_GROUP, SCAN_GROUP, A_WIDTH)
    u = u.reshape(tt // SCAN_GROUP, SCAN_GROUP, A_WIDTH)
    row_in_group = lax.broadcasted_iota(jnp.int32, a.shape, 1)
    s = 1
    while s < SCAN_GROUP:
        keep = row_in_group >= s
        u = a * jnp.where(keep, pltpu.roll(u, s, 1), 0.0) + u
        a = a * jnp.where(keep, pltpu.roll(a, s, 1), 1.0)
        s *= 2
    carry = h_ref[...]
    groups = []
    for g in range(tt // SCAN_GROUP):
        hg = u[g] + a[g] * carry
        carry = hg[SCAN_GROUP - 1:SCAN_GROUP, :]
        groups.append(hg)
    h_ref[...] = carry
    o_ref[...] = (_gelu_tanh(ag_ref[...]) * jnp.concatenate(groups, axis=0)).astype(BF16)


def _rglru(proj3, conv_w, conv_b, wa, ba, wx, bx, lam, l, tt):
    b, s, _ = proj3.shape
    vec = pl.BlockSpec((None, 1, A_WIDTH), lambda bi, ti: (l, 0, 0))
    blk = pl.BlockSpec((None, A_HEADS, A_HEAD_DIM, A_HEAD_DIM), lambda bi, ti: (l, 0, 0, 0))
    return pl.pallas_call(
        functools.partial(_rglru_kernel, tt=tt),
        grid=(b, s // tt),
        in_specs=[
            pl.BlockSpec((None, tt, A_WIDTH), lambda bi, ti: (bi, ti, 0)),
            pl.BlockSpec((None, tt, A_WIDTH), lambda bi, ti: (bi, ti, 1)),
            pl.BlockSpec((None, CONV_WIDTH, A_WIDTH), lambda bi, ti: (l, 0, 0)),
            vec, blk, vec, blk, vec, vec,
        ],
        out_specs=pl.BlockSpec((None, tt, A_WIDTH), lambda bi, ti: (bi, ti, 0)),
        out_shape=jax.ShapeDtypeStruct((b, s, A_WIDTH), BF16),
        scratch_shapes=[pltpu.VMEM((tt + 8, A_WIDTH), F32), pltpu.VMEM((1, A_WIDTH), F32)],
        compiler_params=pltpu.CompilerParams(
            dimension_semantics=("parallel", "arbitrary"), vmem_limit_bytes=VMEM_LIMIT),
        name="rglru",
    )(proj3, proj3, conv_w, conv_b, wa, ba, wx, bx, lam)


def _block_row_bcast(ref, r0, j, lanes):
    d = lanes.stop - lanes.start
    return jnp.concatenate(
        [jnp.broadcast_to(ref[r0 + SUB * blk + j:r0 + SUB * blk + j + 1, lanes], (SUB, d))
         for blk in range(N_SUB)], axis=0)


def _cumulative_decay(g_s, b_s, tt, tri):
    for r0 in range(0, tt, CHUNK):
        rows = slice(r0, r0 + CHUNK)
        g2 = g_s[rows, :]
        g_hi = g2.astype(BF16)
        g_r = g2 - g_hi.astype(F32)
        g_mid = g_r.astype(BF16)
        g_lo = (g_r - g_mid.astype(F32)).astype(BF16)
        b_s[rows, :] = _dot(tri, g_hi) + _dot(tri, g_mid) + _dot(tri, g_lo)


def _gla_state_update(k_s, b_s, r0, lanes, v):
    rows = slice(r0, r0 + CHUNK)
    k, b = k_s[rows, lanes], b_s[rows, lanes]
    khat = k * jnp.exp2(b[CHUNK - 1:CHUNK, :] - b)
    return _dot_tn(v.astype(BF16), khat.astype(BF16))


def _gla_chunk(q_s, k_s, b_s, r0, lanes, v, st_bf, diag_mask, level_masks):
    rows = slice(r0, r0 + CHUNK)
    q, k, b = q_s[rows, lanes], k_s[rows, lanes], b_s[rows, lanes]
    dk = q.shape[1]

    o = _dot_nt((q * jnp.exp2(b)).astype(BF16), st_bf)

    col_in_blk = lax.broadcasted_iota(jnp.int32, (CHUNK, CHUNK), 1) % SUB
    scores = jnp.zeros((CHUNK, CHUNK), F32)
    for j in range(SUB):
        dec = jnp.exp2(b - _block_row_bcast(b_s, r0, j, lanes))
        s_j = jnp.sum(q * _block_row_bcast(k_s, r0, j, lanes) * dec, axis=1, keepdims=True)
        scores = jnp.where(col_in_blk == j, s_j, scores)
    scores = jnp.where(diag_mask, scores, 0.0)

    size = SUB
    for mask in level_masks:
        q_rows, k_rows = [], []
        zeros = jnp.zeros((size, dk), F32)
        for lo in range(0, CHUNK, 2 * size):
            mid, hi = lo + size, lo + 2 * size
            r = b[mid - 1:mid, :]
            k_rows += [k[lo:mid, :] * jnp.exp2(r - b[lo:mid, :]), zeros]
            q_rows += [zeros, q[mid:hi, :] * jnp.exp2(b[mid:hi, :] - r)]
        s_lvl = _dot_nt(jnp.concatenate(q_rows, axis=0).astype(BF16),
                        jnp.concatenate(k_rows, axis=0).astype(BF16))
        scores = scores + (s_lvl if mask is None else jnp.where(mask, s_lvl, 0.0))
        size *= 2
    return o + _dot(scores.astype(BF16), v.astype(BF16))


def _chunk_constants():
    row = lax.broadcasted_iota(jnp.int32, (CHUNK, CHUNK), 0)
    col = lax.broadcasted_iota(jnp.int32, (CHUNK, CHUNK), 1)
    tri = jnp.where(row >= col, 1.0, 0.0).astype(BF16)
    diag_mask = (row // SUB == col // SUB) & (col <= row)
    level_masks = []
    size = SUB
    while 2 * size < CHUNK:
        level_masks.append(row // (2 * size) == col // (2 * size))
        size *= 2
    level_masks.append(None)
    return tri, diag_mask, level_masks


def _gla_scratch(tt, dk, dv):
    hp, n_chunks = HEADS_PER_STEP, tt // CHUNK
    return ([pltpu.VMEM((tt, hp * dk), F32)] * 4
            + [pltpu.VMEM((hp, n_chunks, dv, dk), F32), pltpu.VMEM((hp, n_chunks, dv, dk), BF16),
               pltpu.VMEM((hp, dv, dk), F32)])


def _gla_loop(q_s, k_s, g_s, b_s, u_s, sb_s, v_ref, gate_ref, gate_fn, norm_g, st_ref, o_ref,
              tt, dk, dv):
    tri, diag_mask, level_masks = _chunk_constants()
    heads = [(h, slice(h * dk, (h + 1) * dk), slice(h * dv, (h + 1) * dv))
             for h in range(HEADS_PER_STEP)]
    chunks = list(enumerate(range(0, tt, CHUNK)))

    _cumulative_decay(g_s, b_s, tt, tri)
    for c, r0 in chunks:
        for h, kl, vl in heads:
            u_s[h, c] = _gla_state_update(k_s, b_s, r0, kl, v_ref[r0:r0 + CHUNK, vl])
    for h, kl, vl in heads:
        st = st_ref[h]
        for c, r0 in chunks:
            sb_s[h, c] = st.astype(BF16)
            st = st * jnp.exp2(b_s[r0 + CHUNK - 1:r0 + CHUNK, kl]) + u_s[h, c]
        st_ref[h] = st
    for c, r0 in chunks:
        rows = slice(r0, r0 + CHUNK)
        for h, kl, vl in heads:
            o = _gla_chunk(q_s, k_s, b_s, r0, kl, v_ref[rows, vl], sb_s[h, c],
                           diag_mask, level_masks)
            y = o * lax.rsqrt(jnp.mean(o * o, axis=-1, keepdims=True) + EPS) * norm_g
            o_ref[rows, vl] = (y * gate_fn(gate_ref[rows, vl])).astype(BF16)


def _silu(x):
    return x * _sigmoid(x)


def _gla_b_kernel(q_ref, k_ref, v_ref, gate_ref, lr_ref, w2_ref, b2_ref, ng_ref,
                  o_ref, q_s, k_s, g_s, b_s, u_s, sb_s, st_ref, *, tt):
    @pl.when(pl.program_id(2) == 0)
    def _():
        st_ref[...] = jnp.zeros_like(st_ref)

    q_s[...] = q_ref[...] * (B_DK ** -0.5)
    k_s[...] = k_ref[...]
    z = _dot(lr_ref[...].astype(BF16), w2_ref[...]) + b2_ref[...]
    g_s[...] = _log_sigmoid(z) * (LOG2E / GATE_NORMALIZER)
    _gla_loop(q_s, k_s, g_s, b_s, u_s, sb_s, v_ref, gate_ref, _silu, ng_ref[...], st_ref, o_ref,
              tt, B_DK, B_DV)


def _gla_b(proj3, w2, b2, norm_g, l, tt):
    b, s, _ = proj3.shape
    hp = HEADS_PER_STEP
    tok = lambda width, base: pl.BlockSpec(
        (None, tt, hp * width), lambda bi, h, ti: (bi, ti, base // (hp * width) + h))
    return pl.pallas_call(
        functools.partial(_gla_b_kernel, tt=tt),
        grid=(b, B_HEADS // hp, s // tt),
        in_specs=[
            tok(B_DK, 2048), tok(B_DK, 2560), tok(B_DV, 3072), tok(B_DV, 4096),
            pl.BlockSpec((None, tt, 128), lambda bi, h, ti: (bi, ti, LR_COL // 128)),
            pl.BlockSpec((None, 128, hp * B_DK), lambda bi, h, ti: (l, 0, h)),
            pl.BlockSpec((None, 1, hp * B_DK), lambda bi, h, ti: (l, 0, h)),
            pl.BlockSpec((None, 1, B_DV), lambda bi, h, ti: (l, 0, 0)),
        ],
        out_specs=pl.BlockSpec((None, tt, hp * B_DV), lambda bi, h, ti: (bi, ti, h)),
        out_shape=jax.ShapeDtypeStruct((b, s, B_HEADS * B_DV), BF16),
        scratch_shapes=_gla_scratch(tt, B_DK, B_DV),
        compiler_params=pltpu.CompilerParams(
            dimension_semantics=("parallel", "parallel", "arbitrary"),
            vmem_limit_bytes=VMEM_LIMIT),
        name="gla_b",
    )(proj3, proj3, proj3, proj3, proj3, w2, b2, norm_g)


def _gla_c_kernel(q_ref, f_ref, v_ref, gate_ref, lbl_ref, ng_ref,
                  o_ref, q_s, k_s, g_s, b_s, u_s, sb_s, st_ref, *, tt, layer):
    @pl.when(pl.program_id(2) == 0)
    def _():
        st_ref[...] = jnp.zeros_like(st_ref)

    logits = lbl_ref[...]
    p = jnp.exp(logits - jnp.max(logits, axis=0, keepdims=True))
    p = p / jnp.sum(p, axis=0, keepdims=True)
    lb = jnp.zeros((1, logits.shape[1]), F32)
    for m in range(1, layer + 1):
        lb = lb + p[m:m + 1, :]

    z = f_ref[...]
    q_s[...] = _silu(q_ref[...])
    t = jnp.exp(jnp.minimum(-z, EXP_CLIP))
    r = 1.0 / (1.0 + t)
    g_s[...] = (jnp.log((1.0 + lb * t) * r) + jnp.minimum(z + EXP_CLIP, 0.0)) * LOG2E
    k_s[...] = (1.0 - lb) * (t * r)
    _gla_loop(q_s, k_s, g_s, b_s, u_s, sb_s, v_ref, gate_ref, _sigmoid, ng_ref[...], st_ref, o_ref,
              tt, C_DK, C_DV)


def _gla_c(proj3, lb_logits, norm_g, l, tt):
    b, s, _ = proj3.shape
    depth = lb_logits.shape[0]
    hp = HEADS_PER_STEP
    tok = lambda base: pl.BlockSpec(
        (None, tt, hp * C_DK), lambda bi, h, ti: (bi, ti, base // (hp * C_DK) + h))
    return pl.pallas_call(
        functools.partial(_gla_c_kernel, tt=tt, layer=l),
        grid=(b, C_HEADS // hp, s // tt),
        in_specs=[
            tok(C_COL), tok(C_COL + 1024), tok(C_COL + 2048), tok(C_COL + 3072),
            pl.BlockSpec((depth, hp * C_DK), lambda bi, h, ti: (0, h)),
            pl.BlockSpec((None, 1, C_DV), lambda bi, h, ti: (l, 0, 0)),
        ],
        out_specs=pl.BlockSpec((None, tt, hp * C_DV), lambda bi, h, ti: (bi, ti, h)),
        out_shape=jax.ShapeDtypeStruct((b, s, C_HEADS * C_DV), BF16),
        scratch_shapes=_gla_scratch(tt, C_DK, C_DV),
        compiler_params=pltpu.CompilerParams(
            dimension_semantics=("parallel", "parallel", "arbitrary"),
            vmem_limit_bytes=VMEM_LIMIT),
        name="gla_c",
    )(proj3, proj3, proj3, proj3, lb_logits, norm_g)


def _merge_kernel(h_ref, ya_ref, yb_ref, yc_ref, ga_ref, gb_ref, gc_ref, wa_ref, wb_ref, wc_ref,
                  wo_ref, x_ref, g_ref, o_ref, *, n_n):
    j = pl.program_id(1)

    @pl.when(j == 0)
    def _():
        o_ref[...] = jnp.zeros_like(o_ref)

    h = h_ref[...]
    merged = (_sigmoid(_dot(h, ga_ref[...])) * _dot(ya_ref[...], wa_ref[...])
              + _sigmoid(_dot(h, gb_ref[...])) * _dot(yb_ref[...], wb_ref[...])
              + _sigmoid(_dot(h, gc_ref[...])) * _dot(yc_ref[...], wc_ref[...]))
    o_ref[...] += _dot(merged.astype(BF16), wo_ref[...])

    @pl.when(j == n_n - 1)
    def _():
        o_ref[...] = x_ref[...] + _rms(o_ref[...], g_ref[...])


def _merge(h2d, ya, yb, yc, w_proj, wa, wb, wc, wo, x2d, g, l, tm, tn):
    t, d = x2d.shape
    width = ya.shape[-1]
    n_n = d // tn
    tile = pl.BlockSpec((tm, d), lambda i, j: (i, 0))
    y_spec = pl.BlockSpec((tm, width), lambda i, j: (i, 0))
    gate = lambda k: pl.BlockSpec(
        (None, d, tn), lambda i, j: (l, 0, (MERGE_COL + k * d) // tn + j))
    w_spec = pl.BlockSpec((None, width, tn), lambda i, j: (l, 0, j))
    return pl.pallas_call(
        functools.partial(_merge_kernel, n_n=n_n),
        grid=(t // tm, n_n),
        in_specs=[
            tile, y_spec, y_spec, y_spec, gate(0), gate(1), gate(2), w_spec, w_spec, w_spec,
            pl.BlockSpec((None, tn, d), lambda i, j: (l, j, 0)),
            tile, _gain_spec(l, d),
        ],
        out_specs=tile,
        out_shape=jax.ShapeDtypeStruct((t, d), F32),
        compiler_params=pltpu.CompilerParams(
            dimension_semantics=("parallel", "arbitrary"), vmem_limit_bytes=VMEM_LIMIT_MERGE),
        name="merge",
    )(h2d, ya, yb, yc, w_proj, w_proj, w_proj, wa, wb, wc, wo, x2d, g)


def _relayout_kernel(w_ref, o_ref):
    lr_end = LR_COL + B_RANK
    rows = w_ref.shape[0]
    o_ref[:, :lr_end] = w_ref[:, :lr_end]
    o_ref[:, lr_end:C_COL] = jnp.zeros((rows, C_COL - lr_end), BF16)
    o_ref[:, C_COL:] = w_ref[:, lr_end:]


def _relayout_w_in(w_in, rows):
    depth, d, n = w_in.shape
    return pl.pallas_call(
        _relayout_kernel,
        grid=(depth, d // rows),
        in_specs=[pl.BlockSpec((None, rows, n), lambda l, i: (l, i, 0))],
        out_specs=pl.BlockSpec((None, rows, PROJ_WIDTH), lambda l, i: (l, i, 0)),
        out_shape=jax.ShapeDtypeStruct((depth, d, PROJ_WIDTH), BF16),
        compiler_params=pltpu.CompilerParams(
            dimension_semantics=("parallel", "parallel"), vmem_limit_bytes=VMEM_LIMIT),
        name="relayout",
    )(w_in)


def _tile(n, want):
    t = min(n, want)
    assert n % t == 0, (n, t)
    return t


def kernel(x, ffn1_pre_g, ffn1_w_gate, ffn1_w_up, ffn1_w_down, ffn1_post_g, mix_pre_g, w_in, conv_w, conv_b, lru_w_a, lru_b_a, lru_w_x, lru_b_x, lru_lambda, gla_w2, gla_b2, gla_norm_g, hgrn_lb_logits, hgrn_norm_g, w_br_a, w_br_b, w_br_c, w_out, mix_post_g, ffn2_pre_g, ffn2_w_gate, ffn2_w_up, ffn2_w_down, ffn2_post_g):
    bsz, seq, d = x.shape
    depth = w_in.shape[0]
    t = bsz * seq
    assert d == D_MODEL and seq % CHUNK == 0

    bf = lambda w: w.astype(BF16)
    row = lambda v: v.reshape(depth, 1, v.shape[-1])
    w_proj = _relayout_w_in(bf(w_in), 256)
    w2 = jnp.concatenate(
        [bf(gla_w2), jnp.zeros((depth, 128 - B_RANK, gla_w2.shape[-1]), BF16)], axis=1)
    ffn1 = (row(ffn1_pre_g), bf(ffn1_w_gate), bf(ffn1_w_up), bf(ffn1_w_down), row(ffn1_post_g))
    ffn2 = (row(ffn2_pre_g), bf(ffn2_w_gate), bf(ffn2_w_up), bf(ffn2_w_down), row(ffn2_post_g))
    lru = (conv_w, row(conv_b), bf(lru_w_a), row(lru_b_a), bf(lru_w_x), row(lru_b_x),
           row(lru_lambda))
    br = (bf(w_br_a), bf(w_br_b), bf(w_br_c), bf(w_out))

    tm_ffn, tf = _tile(t, 512), _tile(D_FF, 512)
    tm_proj, tn_proj = _tile(t, 2048), 512
    tm_merge, tn_merge = _tile(t, 512), 512
    tt_lru, tt_gla = _tile(seq, 256), _tile(seq, 1024)

    x2, h1 = x.reshape(t, d), None
    for l in range(depth):
        x2, h_mix = _ffn(x2, h1, *ffn1, row(mix_pre_g), l, l, tm_ffn, tf)
        proj3 = _proj(h_mix, w_proj, l, MERGE_COL, tm_proj, tn_proj).reshape(bsz, seq, MERGE_COL)
        y_a = _rglru(proj3, *lru, l, tt_lru)
        y_b = _gla_b(proj3, w2, row(gla_b2), row(gla_norm_g), l, tt_gla)
        y_c = _gla_c(proj3, hgrn_lb_logits, row(hgrn_norm_g), l, tt_gla)
        x2 = _merge(h_mix, y_a.reshape(t, -1), y_b.reshape(t, -1), y_c.reshape(t, -1), w_proj,
                    *br, x2, row(mix_post_g), l, tm_merge, tn_merge)
        if l + 1 < depth:
            x2, h1 = _ffn(x2, None, *ffn2, ffn1[0], l + 1, l, tm_ffn, tf)
        else:
            x2 = _ffn(x2, None, *ffn2, None, None, l, tm_ffn, tf)
    return x2.reshape(bsz, seq, d)
```

```python
import functools

import numpy as np
import jax
import jax.numpy as jnp
from jax import lax
from jax.experimental import pallas as pl
from jax.experimental.pallas import tpu as pltpu

F32 = jnp.float32
BF16 = jnp.bfloat16

D_MODEL = 2048
D_FF = 5632
A_WIDTH = 1024
A_HEADS = 8
A_HEAD_DIM = 128
CONV_WIDTH = 4
LRU_C = 8.0
B_HEADS = 4
B_DK = 128
B_DV = 256
B_RANK = 16
GATE_NORMALIZER = 16.0
C_HEADS = 8
C_DK = 128
C_DV = 128
EPS = 1e-6
EXP_CLIP = 80.0
LOG2E = 1.4426950408889634

LR_PAD = 512
LR_COL = 5120
PROJ_WIDTH = 15376 - B_RANK + LR_PAD
C_COL = LR_COL + LR_PAD
MERGE_COL = C_COL + 4096

CHUNK = 64
SUB = 8
N_SUB = CHUNK // SUB
HEADS_PER_STEP = 2
SCAN_GROUP = 8

VMEM_LIMIT = 48 * 1024 * 1024
VMEM_LIMIT_MERGE = 58 * 1024 * 1024


def _rms(x, g):
    return x * lax.rsqrt(jnp.mean(x * x, axis=-1, keepdims=True) + EPS) * g


def _sigmoid(x):
    return 0.5 * jnp.tanh(0.5 * x) + 0.5


def _softplus(x):
    return jnp.maximum(x, 0.0) + jnp.log1p(jnp.exp(-jnp.abs(x)))


def _log_sigmoid(x):
    return -_softplus(-x)


def _gelu_tanh(x):
    return 0.5 * x * (1.0 + jnp.tanh(np.sqrt(2.0 / np.pi) * (x + 0.044715 * (x * x * x))))


def _dot(a, b):
    return jnp.dot(a, b, preferred_element_type=F32)


def _dot_nt(a, b):
    return lax.dot_general(a, b, (((1,), (1,)), ((), ())), preferred_element_type=F32)


def _dot_tn(a, b):
    return lax.dot_general(a, b, (((0,), (0,)), ((), ())), preferred_element_type=F32)


def _ffn_kernel(x_ref, pre_ref, wg_ref, wu_ref, wd_ref, post_g_ref, *rest, n_f, h_given,
                emit_norm):
    rest = list(rest)
    next_g_ref = rest.pop(0) if emit_norm else None
    o_ref = rest.pop(0)
    hn_ref = rest.pop(0) if emit_norm else None
    h_ref = pre_ref if h_given else rest.pop(0)
    j = pl.program_id(1)

    @pl.when(j == 0)
    def _():
        if not h_given:
            h_ref[...] = _rms(x_ref[...], pre_ref[...]).astype(BF16)
        o_ref[...] = jnp.zeros_like(o_ref)

    h = h_ref[...]
    tf = wg_ref.shape[1]
    acc = None
    for c0 in range(0, tf, tf // 2):
        cols = slice(c0, c0 + tf // 2)
        g = _dot(h, wg_ref[:, cols])
        u = _dot(h, wu_ref[:, cols])
        act = (g * _sigmoid(g) * u).astype(BF16)
        part = _dot(act, wd_ref[cols, :])
        acc = part if acc is None else acc + part
    o_ref[...] += acc

    @pl.when(j == n_f - 1)
    def _():
        y = x_ref[...] + _rms(o_ref[...], 0.5 * post_g_ref[...])
        o_ref[...] = y
        if emit_norm:
            hn_ref[...] = _rms(y, next_g_ref[...]).astype(BF16)


def _gain_spec(layer, d):
    return pl.BlockSpec((None, 1, d), lambda i, j: (layer, 0, 0))


def _ffn(x2d, h2d, pre_g, wg, wu, wd, post_g, next_g, next_l, l, tm, tf):
    t, d = x2d.shape
    f = wg.shape[-1]
    h_given, emit_norm = h2d is not None, next_g is not None
    tile = pl.BlockSpec((tm, d), lambda i, j: (i, 0))
    return pl.pallas_call(
        functools.partial(_ffn_kernel, n_f=f // tf, h_given=h_given, emit_norm=emit_norm),
        grid=(t // tm, f // tf),
        in_specs=[
            tile, tile if h_given else _gain_spec(l, d),
            pl.BlockSpec((None, d, tf), lambda i, j: (l, 0, j)),
            pl.BlockSpec((None, d, tf), lambda i, j: (l, 0, j)),
            pl.BlockSpec((None, tf, d), lambda i, j: (l, j, 0)),
            _gain_spec(l, d),
        ] + ([_gain_spec(next_l, d)] if emit_norm else []),
        out_specs=(tile, tile) if emit_norm else tile,
        out_shape=((jax.ShapeDtypeStruct((t, d), F32), jax.ShapeDtypeStruct((t, d), BF16))
                   if emit_norm else jax.ShapeDtypeStruct((t, d), F32)),
        scratch_shapes=[] if h_given else [pltpu.VMEM((tm, d), BF16)],
        compiler_params=pltpu.CompilerParams(
            dimension_semantics=("parallel", "arbitrary"), vmem_limit_bytes=VMEM_LIMIT),
        name="ffn",
    )(x2d, h2d if h_given else pre_g, wg, wu, wd, post_g, *((next_g,) if emit_norm else ()))


def _proj_kernel(h_ref, w_ref, o_ref):
    o_ref[...] = _dot(h_ref[...], w_ref[...])


def _proj(h2d, w, l, n, tm, tn):
    t, d = h2d.shape
    return pl.pallas_call(
        _proj_kernel,
        grid=(t // tm, n // tn),
        in_specs=[
            pl.BlockSpec((tm, d), lambda i, j: (i, 0)),
            pl.BlockSpec((None, d, tn), lambda i, j: (l, 0, j)),
        ],
        out_specs=pl.BlockSpec((tm, tn), lambda i, j: (i, j)),
        out_shape=jax.ShapeDtypeStruct((t, n), F32),
        compiler_params=pltpu.CompilerParams(
            dimension_semantics=("parallel", "arbitrary"), vmem_limit_bytes=VMEM_LIMIT),
        name="proj",
    )(h2d, w)


def _rglru_kernel(ax_ref, ag_ref, cw_ref, cb_ref, wa_ref, ba_ref, wx_ref, bx_ref, lam_ref,
                  o_ref, xs_ref, h_ref, *, tt):
    halo = 8

    @pl.when(pl.program_id(1) == 0)
    def _():
        xs_ref[0:halo, :] = jnp.zeros((halo, A_WIDTH), F32)
        h_ref[...] = jnp.zeros_like(h_ref)

    xa = ax_ref[...]
    xs_ref[halo:halo + tt, :] = xa
    cw = cw_ref[...]
    conv = cb_ref[...] + cw[3:4, :] * xa
    for k in range(CONV_WIDTH - 1):
        back = CONV_WIDTH - 1 - k
        conv = conv + cw[k:k + 1, :] * xs_ref[halo - back:halo - back + tt, :]
    xs_ref[0:halo, :] = xs_ref[tt:tt + halo, :]

    conv_bf = conv.astype(BF16)
    r_parts, i_parts = [], []
    for hd in range(A_HEADS):
        xh = conv_bf[:, hd * A_HEAD_DIM:(hd + 1) * A_HEAD_DIM]
        r_parts.append(_dot(xh, wa_ref[hd]))
        i_parts.append(_dot(xh, wx_ref[hd]))
    r = _sigmoid(jnp.concatenate(r_parts, axis=1) + ba_ref[...])
    i = _sigmoid(jnp.concatenate(i_parts, axis=1) + bx_ref[...])
    log_a = (-LRU_C * _softplus(-lam_ref[...])) * r
    a = jnp.exp(log_a)
    th = jnp.tanh(log_a)
    u = jnp.sqrt(-2.0 * th / (1.0 - th)) * (i * conv)

    a = a.reshape(tt // SCAN_GROUP, SCAN_GROUP, A_WIDTH)
    u = u.reshape(tt // SCAN_GROUP, SCAN_GROUP, A_WIDTH)
    row_in_group = lax.broadcasted_iota(jnp.int32, a.shape, 1)
    s = 1
    while s < SCAN_GROUP:
        keep = row_in_group >= s
        u = a * jnp.where(keep, pltpu.roll(u, s, 1), 0.0) + u
        a = a * jnp.where(keep, pltpu.roll(a, s, 1), 1.0)
        s *= 2
    carry = h_ref[...]
    groups = []
    for g in range(tt // SCAN_GROUP):
        hg = u[g] + a[g] * carry
        carry = hg[SCAN_GROUP - 1:SCAN_GROUP, :]
        groups.append(hg)
    h_ref[...] = carry
    o_ref[...] = (_gelu_tanh(ag_ref[...]) * jnp.concatenate(groups, axis=0)).astype(BF16)


def _rglru(proj3, conv_w, conv_b, wa, ba, wx, bx, lam, l, tt):
    b, s, _ = proj3.shape
    vec = pl.BlockSpec((None, 1, A_WIDTH), lambda bi, ti: (l, 0, 0))
    blk = pl.BlockSpec((None, A_HEADS, A_HEAD_DIM, A_HEAD_DIM), lambda bi, ti: (l, 0, 0, 0))
    return pl.pallas_call(
        functools.partial(_rglru_kernel, tt=tt),
        grid=(b, s // tt),
        in_specs=[
            pl.BlockSpec((None, tt, A_WIDTH), lambda bi, ti: (bi, ti, 0)),
            pl.BlockSpec((None, tt, A_WIDTH), lambda bi, ti: (bi, ti, 1)),
            pl.BlockSpec((None, CONV_WIDTH, A_WIDTH), lambda bi, ti: (l, 0, 0)),
            vec, blk, vec, blk, vec, vec,
        ],
        out_specs=pl.BlockSpec((None, tt, A_WIDTH), lambda bi, ti: (bi, ti, 0)),
        out_shape=jax.ShapeDtypeStruct((b, s, A_WIDTH), BF16),
        scratch_shapes=[pltpu.VMEM((tt + 8, A_WIDTH), F32), pltpu.VMEM((1, A_WIDTH), F32)],
        compiler_params=pltpu.CompilerParams(
            dimension_semantics=("parallel", "arbitrary"), vmem_limit_bytes=VMEM_LIMIT),
        name="rglru",
    )(proj3, proj3, conv_w, conv_b, wa, ba, wx, bx, lam)


def _block_row_bcast(ref, r0, j, lanes):
    d = lanes.stop - lanes.start
    return jnp.concatenate(
        [jnp.broadcast_to(ref[r0 + SUB * blk + j:r0 + SUB * blk + j + 1, lanes], (SUB, d))
         for blk in range(N_SUB)], axis=0)


def _cumulative_decay(g_s, b_s, tt, tri):
    for r0 in range(0, tt, CHUNK):
        rows = slice(r0, r0 + CHUNK)
        g2 = g_s[rows, :]
        g_hi = g2.astype(BF16)
        g_r = g2 - g_hi.astype(F32)
        g_mid = g_r.astype(BF16)
        g_lo = (g_r - g_mid.astype(F32)).astype(BF16)
        b_s[rows, :] = _dot(tri, g_hi) + _dot(tri, g_mid) + _dot(tri, g_lo)


def _gla_state_update(k_s, b_s, r0, lanes, v):
    rows = slice(r0, r0 + CHUNK)
    k, b = k_s[rows, lanes], b_s[rows, lanes]
    khat = k * jnp.exp2(b[CHUNK - 1:CHUNK, :] - b)
    return _dot_tn(v.astype(BF16), khat.astype(BF16))


def _gla_chunk(q_s, k_s, b_s, r0, lanes, v, st_bf, diag_mask, level_masks):
    rows = slice(r0, r0 + CHUNK)
    q, k, b = q_s[rows, lanes], k_s[rows, lanes], b_s[rows, lanes]
    dk = q.shape[1]

    o = _dot_nt((q * jnp.exp2(b)).astype(BF16), st_bf)

    col_in_blk = lax.broadcasted_iota(jnp.int32, (CHUNK, CHUNK), 1) % SUB
    scores = jnp.zeros((CHUNK, CHUNK), F32)
    for j in range(SUB):
        dec = jnp.exp2(b - _block_row_bcast(b_s, r0, j, lanes))
        s_j = jnp.sum(q * _block_row_bcast(k_s, r0, j, lanes) * dec, axis=1, keepdims=True)
        scores = jnp.where(col_in_blk == j, s_j, scores)
    scores = jnp.where(diag_mask, scores, 0.0)

    size = SUB
    for mask in level_masks:
        q_rows, k_rows = [], []
        zeros = jnp.zeros((size, dk), F32)
        for lo in range(0, CHUNK, 2 * size):
            mid, hi = lo + size, lo + 2 * size
            r = b[mid - 1:mid, :]
            k_rows += [k[lo:mid, :] * jnp.exp2(r - b[lo:mid, :]), zeros]
            q_rows += [zeros, q[mid:hi, :] * jnp.exp2(b[mid:hi, :] - r)]
        s_lvl = _dot_nt(jnp.concatenate(q_rows, axis=0).astype(BF16),
                        jnp.concatenate(k_rows, axis=0).astype(BF16))
        scores = scores + (s_lvl if mask is None else jnp.where(mask, s_lvl, 0.0))
        size *= 2
    return o + _dot(scores.astype(BF16), v.astype(BF16))


def _chunk_constants():
    row = lax.broadcasted_iota(jnp.int32, (CHUNK, CHUNK), 0)
    col = lax.broadcasted_iota(jnp.int32, (CHUNK, CHUNK), 1)
    tri = jnp.where(row >= col, 1.0, 0.0).astype(BF16)
    diag_mask = (row // SUB == col // SUB) & (col <= row)
    level_masks = []
    size = SUB
    while 2 * size < CHUNK:
        level_masks.append(row // (2 * size) == col // (2 * size))
        size *= 2
    level_masks.append(None)
    return tri, diag_mask, level_masks


def _gla_scratch(tt, dk, dv):
    hp, n_chunks = HEADS_PER_STEP, tt // CHUNK
    return ([pltpu.VMEM((tt, hp * dk), F32)] * 4
            + [pltpu.VMEM((hp, n_chunks, dv, dk), F32), pltpu.VMEM((hp, n_chunks, dv, dk), BF16),
               pltpu.VMEM((hp, dv, dk), F32)])


def _gla_loop(q_s, k_s, g_s, b_s, u_s, sb_s, v_ref, gate_ref, gate_fn, norm_g, st_ref, o_ref,
              tt, dk, dv):
    tri, diag_mask, level_masks = _chunk_constants()
    heads = [(h, slice(h * dk, (h + 1) * dk), slice(h * dv, (h + 1) * dv))
             for h in range(HEADS_PER_STEP)]
    chunks = list(enumerate(range(0, tt, CHUNK)))

    _cumulative_decay(g_s, b_s, tt, tri)
    for c, r0 in chunks:
        for h, kl, vl in heads:
            u_s[h, c] = _gla_state_update(k_s, b_s, r0, kl, v_ref[r0:r0 + CHUNK, vl])
    for h, kl, vl in heads:
        st = st_ref[h]
        for c, r0 in chunks:
            sb_s[h, c] = st.astype(BF16)
            st = st * jnp.exp2(b_s[r0 + CHUNK - 1:r0 + CHUNK, kl]) + u_s[h, c]
        st_ref[h] = st
    for c, r0 in chunks:
        rows = slice(r0, r0 + CHUNK)
        for h, kl, vl in heads:
            o = _gla_chunk(q_s, k_s, b_s, r0, kl, v_ref[rows, vl], sb_s[h, c],
                           diag_mask, level_masks)
            y = o * lax.rsqrt(jnp.mean(o * o, axis=-1, keepdims=True) + EPS) * norm_g
            o_ref[rows, vl] = (y * gate_fn(gate_ref[rows, vl])).astype(BF16)


def _silu(x):
    return x * _sigmoid(x)


def _gla_b_kernel(q_ref, k_ref, v_ref, gate_ref, lr_ref, w2_ref, b2_ref, ng_ref,
                  o_ref, q_s, k_s, g_s, b_s, u_s, sb_s, st_ref, *, tt):
    @pl.when(pl.program_id(2) == 0)
    def _():
        st_ref[...] = jnp.zeros_like(st_ref)

    q_s[...] = q_ref[...] * (B_DK ** -0.5)
    k_s[...] = k_ref[...]
    z = _dot(lr_ref[...].astype(BF16), w2_ref[...]) + b2_ref[...]
    g_s[...] = _log_sigmoid(z) * (LOG2E / GATE_NORMALIZER)
    _gla_loop(q_s, k_s, g_s, b_s, u_s, sb_s, v_ref, gate_ref, _silu, ng_ref[...], st_ref, o_ref,
              tt, B_DK, B_DV)


def _gla_b(proj3, w2, b2, norm_g, l, tt):
    b, s, _ = proj3.shape
    hp = HEADS_PER_STEP
    tok = lambda width, base: pl.BlockSpec(
        (None, tt, hp * width), lambda bi, h, ti: (bi, ti, base // (hp * width) + h))
    return pl.pallas_call(
        functools.partial(_gla_b_kernel, tt=tt),
        grid=(b, B_HEADS // hp, s // tt),
        in_specs=[
            tok(B_DK, 2048), tok(B_DK, 2560), tok(B_DV, 3072), tok(B_DV, 4096),
            pl.BlockSpec((None, tt, 128), lambda bi, h, ti: (bi, ti, LR_COL // 128)),
            pl.BlockSpec((None, 128, hp * B_DK), lambda bi, h, ti: (l, 0, h)),
            pl.BlockSpec((None, 1, hp * B_DK), lambda bi, h, ti: (l, 0, h)),
            pl.BlockSpec((None, 1, B_DV), lambda bi, h, ti: (l, 0, 0)),
        ],
        out_specs=pl.BlockSpec((None, tt, hp * B_DV), lambda bi, h, ti: (bi, ti, h)),
        out_shape=jax.ShapeDtypeStruct((b, s, B_HEADS * B_DV), BF16),
        scratch_shapes=_gla_scratch(tt, B_DK, B_DV),
        compiler_params=pltpu.CompilerParams(
            dimension_semantics=("parallel", "parallel", "arbitrary"),
            vmem_limit_bytes=VMEM_LIMIT),
        name="gla_b",
    )(proj3, proj3, proj3, proj3, proj3, w2, b2, norm_g)


def _gla_c_kernel(q_ref, f_ref, v_ref, gate_ref, lbl_ref, ng_ref,
                  o_ref, q_s, k_s, g_s, b_s, u_s, sb_s, st_ref, *, tt, layer):
    @pl.when(pl.program_id(2) == 0)
    def _():
        st_ref[...] = jnp.zeros_like(st_ref)

    logits = lbl_ref[...]
    p = jnp.exp(logits - jnp.max(logits, axis=0, keepdims=True))
    p = p / jnp.sum(p, axis=0, keepdims=True)
    lb = jnp.zeros((1, logits.shape[1]), F32)
    for m in range(1, layer + 1):
        lb = lb + p[m:m + 1, :]

    z = f_ref[...]
    q_s[...] = _silu(q_ref[...])
    t = jnp.exp(jnp.minimum(-z, EXP_CLIP))
    r = 1.0 / (1.0 + t)
    g_s[...] = (jnp.log((1.0 + lb * t) * r) + jnp.minimum(z + EXP_CLIP, 0.0)) * LOG2E
    k_s[...] = (1.0 - lb) * (t * r)
    _gla_loop(q_s, k_s, g_s, b_s, u_s, sb_s, v_ref, gate_ref, _sigmoid, ng_ref[...], st_ref, o_ref,
              tt, C_DK, C_DV)


def _gla_c(proj3, lb_logits, norm_g, l, tt):
    b, s, _ = proj3.shape
    depth = lb_logits.shape[0]
    hp = HEADS_PER_STEP
    tok = lambda base: pl.BlockSpec(
        (None, tt, hp * C_DK), lambda bi, h, ti: (bi, ti, base // (hp * C_DK) + h))
    return pl.pallas_call(
        functools.partial(_gla_c_kernel, tt=tt, layer=l),
        grid=(b, C_HEADS // hp, s // tt),
        in_specs=[
            tok(C_COL), tok(C_COL + 1024), tok(C_COL + 2048), tok(C_COL + 3072),
            pl.BlockSpec((depth, hp * C_DK), lambda bi, h, ti: (0, h)),
            pl.BlockSpec((None, 1, C_DV), lambda bi, h, ti: (l, 0, 0)),
        ],
        out_specs=pl.BlockSpec((None, tt, hp * C_DV), lambda bi, h, ti: (bi, ti, h)),
        out_shape=jax.ShapeDtypeStruct((b, s, C_HEADS * C_DV), BF16),
        scratch_shapes=_gla_scratch(tt, C_DK, C_DV),
        compiler_params=pltpu.CompilerParams(
            dimension_semantics=("parallel", "parallel", "arbitrary"),
            vmem_limit_bytes=VMEM_LIMIT),
        name="gla_c",
    )(proj3, proj3, proj3, proj3, lb_logits, norm_g)


def _merge_kernel(h_ref, ya_ref, yb_ref, yc_ref, ga_ref, gb_ref, gc_ref, wa_ref, wb_ref, wc_ref,
                  wo_ref, x_ref, g_ref, o_ref, *, n_n):
    j = pl.program_id(1)

    @pl.when(j == 0)
    def _():
        o_ref[...] = jnp.zeros_like(o_ref)

    h = h_ref[...]
    merged = (_sigmoid(_dot(h, ga_ref[...])) * _dot(ya_ref[...], wa_ref[...])
              + _sigmoid(_dot(h, gb_ref[...])) * _dot(yb_ref[...], wb_ref[...])
              + _sigmoid(_dot(h, gc_ref[...])) * _dot(yc_ref[...], wc_ref[...]))
    o_ref[...] += _dot(merged.astype(BF16), wo_ref[...])

    @pl.when(j == n_n - 1)
    def _():
        o_ref[...] = x_ref[...] + _rms(o_ref[...], g_ref[...])


def _merge(h2d, ya, yb, yc, w_proj, wa, wb, wc, wo, x2d, g, l, tm, tn):
    t, d = x2d.shape
    width = ya.shape[-1]
    n_n = d // tn
    tile = pl.BlockSpec((tm, d), lambda i, j: (i, 0))
    y_spec = pl.BlockSpec((tm, width), lambda i, j: (i, 0))
    gate = lambda k: pl.BlockSpec(
        (None, d, tn), lambda i, j: (l, 0, (MERGE_COL + k * d) // tn + j))
    w_spec = pl.BlockSpec((None, width, tn), lambda i, j: (l, 0, j))
    return pl.pallas_call(
        functools.partial(_merge_kernel, n_n=n_n),
        grid=(t // tm, n_n),
        in_specs=[
            tile, y_spec, y_spec, y_spec, gate(0), gate(1), gate(2), w_spec, w_spec, w_spec,
            pl.BlockSpec((None, tn, d), lambda i, j: (l, j, 0)),
            tile, _gain_spec(l, d),
        ],
        out_specs=tile,
        out_shape=jax.ShapeDtypeStruct((t, d), F32),
        compiler_params=pltpu.CompilerParams(
            dimension_semantics=("parallel", "arbitrary"), vmem_limit_bytes=VMEM_LIMIT_MERGE),
        name="merge",
    )(h2d, ya, yb, yc, w_proj, w_proj, w_proj, wa, wb, wc, wo, x2d, g)


def _tile(n, want):
    t = min(n, want)
    assert n % t == 0, (n, t)
    return t


def kernel(x, ffn1_pre_g, ffn1_w_gate, ffn1_w_up, ffn1_w_down, ffn1_post_g, mix_pre_g, w_in, conv_w, conv_b, lru_w_a, lru_b_a, lru_w_x, lru_b_x, lru_lambda, gla_w2, gla_b2, gla_norm_g, hgrn_lb_logits, hgrn_norm_g, w_br_a, w_br_b, w_br_c, w_out, mix_post_g, ffn2_pre_g, ffn2_w_gate, ffn2_w_up, ffn2_w_down, ffn2_post_g):
    bsz, seq, d = x.shape
    depth = w_in.shape[0]
    t = bsz * seq
    assert d == D_MODEL and seq % CHUNK == 0

    bf = lambda w: w.astype(BF16)
    row = lambda v: v.reshape(depth, 1, v.shape[-1])
    w_proj = jnp.concatenate(
        [bf(w_in[:, :, :LR_COL + B_RANK]),
         jnp.zeros((depth, d, LR_PAD - B_RANK), BF16),
         bf(w_in[:, :, LR_COL + B_RANK:])], axis=-1)
    w2 = jnp.concatenate(
        [bf(gla_w2), jnp.zeros((depth, 128 - B_RANK, gla_w2.shape[-1]), BF16)], axis=1)
    ffn1 = (row(ffn1_pre_g), bf(ffn1_w_gate), bf(ffn1_w_up), bf(ffn1_w_down), row(ffn1_post_g))
    ffn2 = (row(ffn2_pre_g), bf(ffn2_w_gate), bf(ffn2_w_up), bf(ffn2_w_down), row(ffn2_post_g))
    lru = (conv_w, row(conv_b), bf(lru_w_a), row(lru_b_a), bf(lru_w_x), row(lru_b_x),
           row(lru_lambda))
    br = (bf(w_br_a), bf(w_br_b), bf(w_br_c), bf(w_out))

    tm_ffn, tf = _tile(t, 512), _tile(D_FF, 512)
    tm_proj, tn_proj = _tile(t, 2048), 512
    tm_merge, tn_merge = _tile(t, 512), 512
    tt_lru, tt_gla = _tile(seq, 512), _tile(seq, 1024)

    x2, h1 = x.reshape(t, d), None
    for l in range(depth):
        x2, h_mix = _ffn(x2, h1, *ffn1, row(mix_pre_g), l, l, tm_ffn, tf)
        proj3 = _proj(h_mix, w_proj, l, MERGE_COL, tm_proj, tn_proj).reshape(bsz, seq, MERGE_COL)
        y_a = _rglru(proj3, *lru, l, tt_lru)
        y_b = _gla_b(proj3, w2, row(gla_b2), row(gla_norm_g), l, tt_gla)
        y_c = _gla_c(proj3, hgrn_lb_logits, row(hgrn_norm_g), l, tt_gla)
        x2 = _merge(h_mix, y_a.reshape(t, -1), y_b.reshape(t, -1), y_c.reshape(t, -1), w_proj,
                    *br, x2, row(mix_post_g), l, tm_merge, tn_merge)
        if l + 1 < depth:
            x2, h1 = _ffn(x2, None, *ffn2, ffn1[0], l + 1, l, tm_ffn, tf)
        else:
            x2 = _ffn(x2, None, *ffn2, None, None, l, tm_ffn, tf)
    return x2.reshape(bsz, seq, d)
```

```python
import functools

import numpy as np
import jax
import jax.numpy as jnp
from jax import lax
from jax.experimental import pallas as pl
from jax.experimental.pallas import tpu as pltpu

F32 = jnp.float32
BF16 = jnp.bfloat16

D_MODEL = 2048
D_FF = 5632
A_WIDTH = 1024
A_HEADS = 8
A_HEAD_DIM = 128
CONV_WIDTH = 4
LRU_C = 8.0
B_HEADS = 4
B_DK = 128
B_DV = 256
B_RANK = 16
GATE_NORMALIZER = 16.0
C_HEADS = 8
C_DK = 128
C_DV = 128
EPS = 1e-6
EXP_CLIP = 80.0
LOG2E = 1.4426950408889634

LR_PAD = 512
LR_COL = 5120
PROJ_WIDTH = 15376 - B_RANK + LR_PAD
C_COL = LR_COL + LR_PAD
MERGE_COL = C_COL + 4096

CHUNK = 64
SUB = 8
N_SUB = CHUNK // SUB
HEADS_PER_STEP = 2
SCAN_GROUP = 8

VMEM_LIMIT = 48 * 1024 * 1024
VMEM_LIMIT_MERGE = 58 * 1024 * 1024


def _rms(x, g):
    return x * lax.rsqrt(jnp.mean(x * x, axis=-1, keepdims=True) + EPS) * g


def _sigmoid(x):
    return 0.5 * jnp.tanh(0.5 * x) + 0.5


def _softplus(x):
    return jnp.maximum(x, 0.0) + jnp.log1p(jnp.exp(-jnp.abs(x)))


def _log_sigmoid(x):
    return -_softplus(-x)


def _gelu_tanh(x):
    return 0.5 * x * (1.0 + jnp.tanh(np.sqrt(2.0 / np.pi) * (x + 0.044715 * (x * x * x))))


def _dot(a, b):
    return jnp.dot(a, b, preferred_element_type=F32)


def _dot_nt(a, b):
    return lax.dot_general(a, b, (((1,), (1,)), ((), ())), preferred_element_type=F32)


def _dot_tn(a, b):
    return lax.dot_general(a, b, (((0,), (0,)), ((), ())), preferred_element_type=F32)


def _ffn_kernel(x_ref, pre_g_ref, wg_ref, wu_ref, wd_ref, post_g_ref, *rest, n_f, emit_norm):
    if emit_norm:
        next_g_ref, o_ref, hn_ref, h_ref = rest
    else:
        o_ref, h_ref = rest
    j = pl.program_id(1)

    @pl.when(j == 0)
    def _():
        h_ref[...] = _rms(x_ref[...], pre_g_ref[...]).astype(BF16)
        o_ref[...] = jnp.zeros_like(o_ref)

    h = h_ref[...]
    tf = wg_ref.shape[1]
    acc = None
    for c0 in range(0, tf, tf // 2):
        cols = slice(c0, c0 + tf // 2)
        g = _dot(h, wg_ref[:, cols])
        u = _dot(h, wu_ref[:, cols])
        act = (g * _sigmoid(g) * u).astype(BF16)
        part = _dot(act, wd_ref[cols, :])
        acc = part if acc is None else acc + part
    o_ref[...] += acc

    @pl.when(j == n_f - 1)
    def _():
        y = x_ref[...] + _rms(o_ref[...], 0.5 * post_g_ref[...])
        o_ref[...] = y
        if emit_norm:
            hn_ref[...] = _rms(y, next_g_ref[...]).astype(BF16)


def _ffn(x2d, pre_g, wg, wu, wd, post_g, next_g, l, tm, tf):
    t, d = x2d.shape
    f = wg.shape[-1]
    emit_norm = next_g is not None
    gain = pl.BlockSpec((None, 1, d), lambda i, j: (l, 0, 0))
    tile = pl.BlockSpec((tm, d), lambda i, j: (i, 0))
    return pl.pallas_call(
        functools.partial(_ffn_kernel, n_f=f // tf, emit_norm=emit_norm),
        grid=(t // tm, f // tf),
        in_specs=[
            tile, gain,
            pl.BlockSpec((None, d, tf), lambda i, j: (l, 0, j)),
            pl.BlockSpec((None, d, tf), lambda i, j: (l, 0, j)),
            pl.BlockSpec((None, tf, d), lambda i, j: (l, j, 0)),
            gain,
        ] + ([gain] if emit_norm else []),
        out_specs=(tile, tile) if emit_norm else tile,
        out_shape=((jax.ShapeDtypeStruct((t, d), F32), jax.ShapeDtypeStruct((t, d), BF16))
                   if emit_norm else jax.ShapeDtypeStruct((t, d), F32)),
        scratch_shapes=[pltpu.VMEM((tm, d), BF16)],
        compiler_params=pltpu.CompilerParams(
            dimension_semantics=("parallel", "arbitrary"), vmem_limit_bytes=VMEM_LIMIT),
        name="ffn",
    )(x2d, pre_g, wg, wu, wd, post_g, *((next_g,) if emit_norm else ()))


def _proj_kernel(h_ref, w_ref, o_ref):
    o_ref[...] = _dot(h_ref[...], w_ref[...])


def _proj(h2d, w, l, n, tm, tn):
    t, d = h2d.shape
    return pl.pallas_call(
        _proj_kernel,
        grid=(t // tm, n // tn),
        in_specs=[
            pl.BlockSpec((tm, d), lambda i, j: (i, 0)),
            pl.BlockSpec((None, d, tn), lambda i, j: (l, 0, j)),
        ],
        out_specs=pl.BlockSpec((tm, tn), lambda i, j: (i, j)),
        out_shape=jax.ShapeDtypeStruct((t, n), F32),
        compiler_params=pltpu.CompilerParams(
            dimension_semantics=("parallel", "arbitrary"), vmem_limit_bytes=VMEM_LIMIT),
        name="proj",
    )(h2d, w)


def _rglru_kernel(ax_ref, ag_ref, cw_ref, cb_ref, wa_ref, ba_ref, wx_ref, bx_ref, lam_ref,
                  o_ref, xs_ref, h_ref, *, tt):
    halo = 8

    @pl.when(pl.program_id(1) == 0)
    def _():
        xs_ref[0:halo, :] = jnp.zeros((halo, A_WIDTH), F32)
        h_ref[...] = jnp.zeros_like(h_ref)

    xa = ax_ref[...]
    xs_ref[halo:halo + tt, :] = xa
    cw = cw_ref[...]
    conv = cb_ref[...] + cw[3:4, :] * xa
    for k in range(CONV_WIDTH - 1):
        back = CONV_WIDTH - 1 - k
        conv = conv + cw[k:k + 1, :] * xs_ref[halo - back:halo - back + tt, :]
    xs_ref[0:halo, :] = xs_ref[tt:tt + halo, :]

    conv_bf = conv.astype(BF16)
    r_parts, i_parts = [], []
    for hd in range(A_HEADS):
        xh = conv_bf[:, hd * A_HEAD_DIM:(hd + 1) * A_HEAD_DIM]
        r_parts.append(_dot(xh, wa_ref[hd]))
        i_parts.append(_dot(xh, wx_ref[hd]))
    r = _sigmoid(jnp.concatenate(r_parts, axis=1) + ba_ref[...])
    i = _sigmoid(jnp.concatenate(i_parts, axis=1) + bx_ref[...])
    log_a = (-LRU_C * _softplus(-lam_ref[...])) * r
    a = jnp.exp(log_a)
    th = jnp.tanh(log_a)
    u = jnp.sqrt(-2.0 * th / (1.0 - th)) * (i * conv)

    a = a.reshape(tt // SCAN_GROUP, SCAN_GROUP, A_WIDTH)
    u = u.reshape(tt // SCAN_GROUP, SCAN_GROUP, A_WIDTH)
    row_in_group = lax.broadcasted_iota(jnp.int32, a.shape, 1)
    s = 1
    while s < SCAN_GROUP:
        keep = row_in_group >= s
        u = a * jnp.where(keep, pltpu.roll(u, s, 1), 0.0) + u
        a = a * jnp.where(keep, pltpu.roll(a, s, 1), 1.0)
        s *= 2
    carry = h_ref[...]
    groups = []
    for g in range(tt // SCAN_GROUP):
        hg = u[g] + a[g] * carry
        carry = hg[SCAN_GROUP - 1:SCAN_GROUP, :]
        groups.append(hg)
    h_ref[...] = carry
    o_ref[...] = (_gelu_tanh(ag_ref[...]) * jnp.concatenate(groups, axis=0)).astype(BF16)


def _rglru(proj3, conv_w, conv_b, wa, ba, wx, bx, lam, l, tt):
    b, s, _ = proj3.shape
    vec = pl.BlockSpec((None, 1, A_WIDTH), lambda bi, ti: (l, 0, 0))
    blk = pl.BlockSpec((None, A_HEADS, A_HEAD_DIM, A_HEAD_DIM), lambda bi, ti: (l, 0, 0, 0))
    return pl.pallas_call(
        functools.partial(_rglru_kernel, tt=tt),
        grid=(b, s // tt),
        in_specs=[
            pl.BlockSpec((None, tt, A_WIDTH), lambda bi, ti: (bi, ti, 0)),
            pl.BlockSpec((None, tt, A_WIDTH), lambda bi, ti: (bi, ti, 1)),
            pl.BlockSpec((None, CONV_WIDTH, A_WIDTH), lambda bi, ti: (l, 0, 0)),
            vec, blk, vec, blk, vec, vec,
        ],
        out_specs=pl.BlockSpec((None, tt, A_WIDTH), lambda bi, ti: (bi, ti, 0)),
        out_shape=jax.ShapeDtypeStruct((b, s, A_WIDTH), BF16),
        scratch_shapes=[pltpu.VMEM((tt + 8, A_WIDTH), F32), pltpu.VMEM((1, A_WIDTH), F32)],
        compiler_params=pltpu.CompilerParams(
            dimension_semantics=("parallel", "arbitrary"), vmem_limit_bytes=VMEM_LIMIT),
        name="rglru",
    )(proj3, proj3, conv_w, conv_b, wa, ba, wx, bx, lam)


def _block_row_bcast(ref, r0, j, lanes):
    d = lanes.stop - lanes.start
    return jnp.concatenate(
        [jnp.broadcast_to(ref[r0 + SUB * blk + j:r0 + SUB * blk + j + 1, lanes], (SUB, d))
         for blk in range(N_SUB)], axis=0)


def _cumulative_decay(g_s, b_s, tt, tri):
    for r0 in range(0, tt, CHUNK):
        rows = slice(r0, r0 + CHUNK)
        g2 = g_s[rows, :]
        g_hi = g2.astype(BF16)
        g_r = g2 - g_hi.astype(F32)
        g_mid = g_r.astype(BF16)
        g_lo = (g_r - g_mid.astype(F32)).astype(BF16)
        b_s[rows, :] = _dot(tri, g_hi) + _dot(tri, g_mid) + _dot(tri, g_lo)


def _gla_state_update(k_s, b_s, r0, lanes, v):
    rows = slice(r0, r0 + CHUNK)
    k, b = k_s[rows, lanes], b_s[rows, lanes]
    khat = k * jnp.exp2(b[CHUNK - 1:CHUNK, :] - b)
    return _dot_tn(v.astype(BF16), khat.astype(BF16))


def _gla_chunk(q_s, k_s, b_s, r0, lanes, v, st_bf, diag_mask, level_masks):
    rows = slice(r0, r0 + CHUNK)
    q, k, b = q_s[rows, lanes], k_s[rows, lanes], b_s[rows, lanes]
    dk = q.shape[1]

    o = _dot_nt((q * jnp.exp2(b)).astype(BF16), st_bf)

    col_in_blk = lax.broadcasted_iota(jnp.int32, (CHUNK, CHUNK), 1) % SUB
    scores = jnp.zeros((CHUNK, CHUNK), F32)
    for j in range(SUB):
        dec = jnp.exp2(b - _block_row_bcast(b_s, r0, j, lanes))
        s_j = jnp.sum(q * _block_row_bcast(k_s, r0, j, lanes) * dec, axis=1, keepdims=True)
        scores = jnp.where(col_in_blk == j, s_j, scores)
    scores = jnp.where(diag_mask, scores, 0.0)

    size = SUB
    for mask in level_masks:
        q_rows, k_rows = [], []
        zeros = jnp.zeros((size, dk), F32)
        for lo in range(0, CHUNK, 2 * size):
            mid, hi = lo + size, lo + 2 * size
            r = b[mid - 1:mid, :]
            k_rows += [k[lo:mid, :] * jnp.exp2(r - b[lo:mid, :]), zeros]
            q_rows += [zeros, q[mid:hi, :] * jnp.exp2(b[mid:hi, :] - r)]
        s_lvl = _dot_nt(jnp.concatenate(q_rows, axis=0).astype(BF16),
                        jnp.concatenate(k_rows, axis=0).astype(BF16))
        scores = scores + (s_lvl if mask is None else jnp.where(mask, s_lvl, 0.0))
        size *= 2
    return o + _dot(scores.astype(BF16), v.astype(BF16))


def _chunk_constants():
    row = lax.broadcasted_iota(jnp.int32, (CHUNK, CHUNK), 0)
    col = lax.broadcasted_iota(jnp.int32, (CHUNK, CHUNK), 1)
    tri = jnp.where(row >= col, 1.0, 0.0).astype(BF16)
    diag_mask = (row // SUB == col // SUB) & (col <= row)
    level_masks = []
    size = SUB
    while 2 * size < CHUNK:
        level_masks.append(row // (2 * size) == col // (2 * size))
        size *= 2
    level_masks.append(None)
    return tri, diag_mask, level_masks


def _gla_scratch(tt, dk, dv):
    hp, n_chunks = HEADS_PER_STEP, tt // CHUNK
    return ([pltpu.VMEM((tt, hp * dk), F32)] * 4
            + [pltpu.VMEM((hp, n_chunks, dv, dk), F32), pltpu.VMEM((hp, n_chunks, dv, dk), BF16),
               pltpu.VMEM((hp, dv, dk), F32)])


def _gla_loop(q_s, k_s, g_s, b_s, u_s, sb_s, v_ref, gate_ref, gate_fn, norm_g, st_ref, o_ref,
              tt, dk, dv):
    tri, diag_mask, level_masks = _chunk_constants()
    heads = [(h, slice(h * dk, (h + 1) * dk), slice(h * dv, (h + 1) * dv))
             for h in range(HEADS_PER_STEP)]
    chunks = list(enumerate(range(0, tt, CHUNK)))

    _cumulative_decay(g_s, b_s, tt, tri)
    for c, r0 in chunks:
        for h, kl, vl in heads:
            u_s[h, c] = _gla_state_update(k_s, b_s, r0, kl, v_ref[r0:r0 + CHUNK, vl])
    for h, kl, vl in heads:
        st = st_ref[h]
        for c, r0 in chunks:
            sb_s[h, c] = st.astype(BF16)
            st = st * jnp.exp2(b_s[r0 + CHUNK - 1:r0 + CHUNK, kl]) + u_s[h, c]
        st_ref[h] = st
    for c, r0 in chunks:
        rows = slice(r0, r0 + CHUNK)
        for h, kl, vl in heads:
            o = _gla_chunk(q_s, k_s, b_s, r0, kl, v_ref[rows, vl], sb_s[h, c],
                           diag_mask, level_masks)
            y = o * lax.rsqrt(jnp.mean(o * o, axis=-1, keepdims=True) + EPS) * norm_g
            o_ref[rows, vl] = (y * gate_fn(gate_ref[rows, vl])).astype(BF16)


def _silu(x):
    return x * _sigmoid(x)


def _gla_b_kernel(q_ref, k_ref, v_ref, gate_ref, lr_ref, w2_ref, b2_ref, ng_ref,
                  o_ref, q_s, k_s, g_s, b_s, u_s, sb_s, st_ref, *, tt):
    @pl.when(pl.program_id(2) == 0)
    def _():
        st_ref[...] = jnp.zeros_like(st_ref)

    q_s[...] = q_ref[...] * (B_DK ** -0.5)
    k_s[...] = k_ref[...]
    z = _dot(lr_ref[...].astype(BF16), w2_ref[...]) + b2_ref[...]
    g_s[...] = _log_sigmoid(z) * (LOG2E / GATE_NORMALIZER)
    _gla_loop(q_s, k_s, g_s, b_s, u_s, sb_s, v_ref, gate_ref, _silu, ng_ref[...], st_ref, o_ref,
              tt, B_DK, B_DV)


def _gla_b(proj3, w2, b2, norm_g, l, tt):
    b, s, _ = proj3.shape
    hp = HEADS_PER_STEP
    tok = lambda width, base: pl.BlockSpec(
        (None, tt, hp * width), lambda bi, h, ti: (bi, ti, base // (hp * width) + h))
    return pl.pallas_call(
        functools.partial(_gla_b_kernel, tt=tt),
        grid=(b, B_HEADS // hp, s // tt),
        in_specs=[
            tok(B_DK, 2048), tok(B_DK, 2560), tok(B_DV, 3072), tok(B_DV, 4096),
            pl.BlockSpec((None, tt, 128), lambda bi, h, ti: (bi, ti, LR_COL // 128)),
            pl.BlockSpec((None, 128, hp * B_DK), lambda bi, h, ti: (l, 0, h)),
            pl.BlockSpec((None, 1, hp * B_DK), lambda bi, h, ti: (l, 0, h)),
            pl.BlockSpec((None, 1, B_DV), lambda bi, h, ti: (l, 0, 0)),
        ],
        out_specs=pl.BlockSpec((None, tt, hp * B_DV), lambda bi, h, ti: (bi, ti, h)),
        out_shape=jax.ShapeDtypeStruct((b, s, B_HEADS * B_DV), BF16),
        scratch_shapes=_gla_scratch(tt, B_DK, B_DV),
        compiler_params=pltpu.CompilerParams(
            dimension_semantics=("parallel", "parallel", "arbitrary"),
            vmem_limit_bytes=VMEM_LIMIT),
        name="gla_b",
    )(proj3, proj3, proj3, proj3, proj3, w2, b2, norm_g)


def _gla_c_kernel(q_ref, f_ref, v_ref, gate_ref, lbl_ref, ng_ref,
                  o_ref, q_s, k_s, g_s, b_s, u_s, sb_s, st_ref, *, tt, layer):
    @pl.when(pl.program_id(2) == 0)
    def _():
        st_ref[...] = jnp.zeros_like(st_ref)

    logits = lbl_ref[...]
    p = jnp.exp(logits - jnp.max(logits, axis=0, keepdims=True))
    p = p / jnp.sum(p, axis=0, keepdims=True)
    lb = jnp.zeros((1, logits.shape[1]), F32)
    for m in range(1, layer + 1):
        lb = lb + p[m:m + 1, :]

    z = f_ref[...]
    q_s[...] = _silu(q_ref[...])
    t = jnp.exp(jnp.minimum(-z, EXP_CLIP))
    r = 1.0 / (1.0 + t)
    g_s[...] = (jnp.log((1.0 + lb * t) * r) + jnp.minimum(z + EXP_CLIP, 0.0)) * LOG2E
    k_s[...] = (1.0 - lb) * (t * r)
    _gla_loop(q_s, k_s, g_s, b_s, u_s, sb_s, v_ref, gate_ref, _sigmoid, ng_ref[...], st_ref, o_ref,
              tt, C_DK, C_DV)


def _gla_c(proj3, lb_logits, norm_g, l, tt):
    b, s, _ = proj3.shape
    depth = lb_logits.shape[0]
    hp = HEADS_PER_STEP
    tok = lambda base: pl.BlockSpec(
        (None, tt, hp * C_DK), lambda bi, h, ti: (bi, ti, base // (hp * C_DK) + h))
    return pl.pallas_call(
        functools.partial(_gla_c_kernel, tt=tt, layer=l),
        grid=(b, C_HEADS // hp, s // tt),
        in_specs=[
            tok(C_COL), tok(C_COL + 1024), tok(C_COL + 2048), tok(C_COL + 3072),
            pl.BlockSpec((depth, hp * C_DK), lambda bi, h, ti: (0, h)),
            pl.BlockSpec((None, 1, C_DV), lambda bi, h, ti: (l, 0, 0)),
        ],
        out_specs=pl.BlockSpec((None, tt, hp * C_DV), lambda bi, h, ti: (bi, ti, h)),
        out_shape=jax.ShapeDtypeStruct((b, s, C_HEADS * C_DV), BF16),
        scratch_shapes=_gla_scratch(tt, C_DK, C_DV),
        compiler_params=pltpu.CompilerParams(
            dimension_semantics=("parallel", "parallel", "arbitrary"),
            vmem_limit_bytes=VMEM_LIMIT),
        name="gla_c",
    )(proj3, proj3, proj3, proj3, lb_logits, norm_g)


def _merge_kernel(h_ref, ya_ref, yb_ref, yc_ref, ga_ref, gb_ref, gc_ref, wa_ref, wb_ref, wc_ref,
                  wo_ref, x_ref, g_ref, o_ref, *, n_n):
    j = pl.program_id(1)

    @pl.when(j == 0)
    def _():
        o_ref[...] = jnp.zeros_like(o_ref)

    h = h_ref[...]
    merged = (_sigmoid(_dot(h, ga_ref[...])) * _dot(ya_ref[...], wa_ref[...])
              + _sigmoid(_dot(h, gb_ref[...])) * _dot(yb_ref[...], wb_ref[...])
              + _sigmoid(_dot(h, gc_ref[...])) * _dot(yc_ref[...], wc_ref[...]))
    o_ref[...] += _dot(merged.astype(BF16), wo_ref[...])

    @pl.when(j == n_n - 1)
    def _():
        o_ref[...] = x_ref[...] + _rms(o_ref[...], g_ref[...])


def _merge(h2d, ya, yb, yc, w_proj, wa, wb, wc, wo, x2d, g, l, tm, tn):
    t, d = x2d.shape
    width = ya.shape[-1]
    n_n = d // tn
    tile = pl.BlockSpec((tm, d), lambda i, j: (i, 0))
    y_spec = pl.BlockSpec((tm, width), lambda i, j: (i, 0))
    gate = lambda k: pl.BlockSpec(
        (None, d, tn), lambda i, j: (l, 0, (MERGE_COL + k * d) // tn + j))
    w_spec = pl.BlockSpec((None, width, tn), lambda i, j: (l, 0, j))
    return pl.pallas_call(
        functools.partial(_merge_kernel, n_n=n_n),
        grid=(t // tm, n_n),
        in_specs=[
            tile, y_spec, y_spec, y_spec, gate(0), gate(1), gate(2), w_spec, w_spec, w_spec,
            pl.BlockSpec((None, tn, d), lambda i, j: (l, j, 0)),
            tile,
            pl.BlockSpec((None, 1, d), lambda i, j: (l, 0, 0)),
        ],
        out_specs=tile,
        out_shape=jax.ShapeDtypeStruct((t, d), F32),
        compiler_params=pltpu.CompilerParams(
            dimension_semantics=("parallel", "arbitrary"), vmem_limit_bytes=VMEM_LIMIT_MERGE),
        name="merge",
    )(h2d, ya, yb, yc, w_proj, w_proj, w_proj, wa, wb, wc, wo, x2d, g)


def _relayout_kernel(w_ref, o_ref):
    lr_end = LR_COL + B_RANK
    rows = w_ref.shape[0]
    o_ref[:, :lr_end] = w_ref[:, :lr_end].astype(BF16)
    o_ref[:, lr_end:C_COL] = jnp.zeros((rows, C_COL - lr_end), BF16)
    o_ref[:, C_COL:] = w_ref[:, lr_end:].astype(BF16)


def _relayout_w_in(w_in, rows):
    depth, d, n = w_in.shape
    return pl.pallas_call(
        _relayout_kernel,
        grid=(depth, d // rows),
        in_specs=[pl.BlockSpec((None, rows, n), lambda l, i: (l, i, 0))],
        out_specs=pl.BlockSpec((None, rows, PROJ_WIDTH), lambda l, i: (l, i, 0)),
        out_shape=jax.ShapeDtypeStruct((depth, d, PROJ_WIDTH), BF16),
        compiler_params=pltpu.CompilerParams(
            dimension_semantics=("parallel", "parallel"), vmem_limit_bytes=VMEM_LIMIT),
        name="relayout",
    )(w_in)


def _tile(n, want):
    t = min(n, want)
    assert n % t == 0, (n, t)
    return t


def kernel(x, ffn1_pre_g, ffn1_w_gate, ffn1_w_up, ffn1_w_down, ffn1_post_g, mix_pre_g, w_in, conv_w, conv_b, lru_w_a, lru_b_a, lru_w_x, lru_b_x, lru_lambda, gla_w2, gla_b2, gla_norm_g, hgrn_lb_logits, hgrn_norm_g, w_br_a, w_br_b, w_br_c, w_out, mix_post_g, ffn2_pre_g, ffn2_w_gate, ffn2_w_up, ffn2_w_down, ffn2_post_g):
    bsz, seq, d = x.shape
    depth = w_in.shape[0]
    t = bsz * seq
    assert d == D_MODEL and seq % CHUNK == 0

    bf = lambda w: w.astype(BF16)
    row = lambda v: v.reshape(depth, 1, v.shape[-1])
    w_proj = _relayout_w_in(w_in, 128)
    w2 = jnp.concatenate(
        [bf(gla_w2), jnp.zeros((depth, 128 - B_RANK, gla_w2.shape[-1]), BF16)], axis=1)
    ffn1 = (row(ffn1_pre_g), bf(ffn1_w_gate), bf(ffn1_w_up), bf(ffn1_w_down), row(ffn1_post_g))
    ffn2 = (row(ffn2_pre_g), bf(ffn2_w_gate), bf(ffn2_w_up), bf(ffn2_w_down), row(ffn2_post_g))
    lru = (conv_w, row(conv_b), bf(lru_w_a), row(lru_b_a), bf(lru_w_x), row(lru_b_x),
           row(lru_lambda))
    br = (bf(w_br_a), bf(w_br_b), bf(w_br_c), bf(w_out))

    tm_ffn, tf = _tile(t, 512), _tile(D_FF, 512)
    tm_proj, tn_proj = _tile(t, 2048), 512
    tm_merge, tn_merge = _tile(t, 512), 512
    tt_lru, tt_gla = _tile(seq, 512), _tile(seq, 1024)

    x2 = x.reshape(t, d)
    for l in range(depth):
        x2, h2 = _ffn(x2, *ffn1, row(mix_pre_g), l, tm_ffn, tf)
        proj3 = _proj(h2, w_proj, l, MERGE_COL, tm_proj, tn_proj).reshape(bsz, seq, MERGE_COL)
        y_a = _rglru(proj3, *lru, l, tt_lru)
        y_b = _gla_b(proj3, w2, row(gla_b2), row(gla_norm_g), l, tt_gla)
        y_c = _gla_c(proj3, hgrn_lb_logits, row(hgrn_norm_g), l, tt_gla)
        x2 = _merge(h2, y_a.reshape(t, -1), y_b.reshape(t, -1), y_c.reshape(t, -1), w_proj,
                    *br, x2, row(mix_post_g), l, tm_merge, tn_merge)
        x2 = _ffn(x2, *ffn2, None, l, tm_ffn, tf)
    return x2.reshape(bsz, seq, d)
```
